```python
import jax
import jax.numpy as jnp
from jax import lax
import numpy as np

D_MODEL = 2048
BATCH = 8
SEQ = 4096
DEPTH = 4

GRID_W = 64
CTX_LEN = 256
N_EVEN = (DEPTH + 1) // 2
N_ODD = DEPTH // 2
HEAD_DIM = 64
MIX_W = D_MODEL
MOD_CHUNKS = 6

RWKV_WIDTH = MIX_W // 2
RWKV_HEADS = RWKV_WIDTH // HEAD_DIM
RWKV_LORA_W = 64
RWKV_LORA_A = 64
RWKV_LORA_G = 128
RWKV_GN_EPS = 64e-5
RWKV_SHIFT_W = 3 * RWKV_WIDTH + RWKV_LORA_W + RWKV_LORA_A
RWKV_IN = 3 * RWKV_WIDTH + 2 * RWKV_LORA_W + 2 * RWKV_LORA_A + RWKV_LORA_G
CONV_WIDTH = MIX_W - RWKV_WIDTH
CONV_KERNEL = 31
EVEN_IN = RWKV_IN + 2 * CONV_WIDTH
ATT_HEADS = 16
ATT_KV_HEADS = 4
ATT_GROUP = ATT_HEADS // ATT_KV_HEADS
ATT_WIDTH = ATT_HEADS * HEAD_DIM
ATT_KV_WIDTH = ATT_KV_HEADS * HEAD_DIM
Q_BLOCK = 128
ROPE_THETA = 10000.0
ROPE_AXIS_PAIRS = HEAD_DIM // 4
LRU_WIDTH = MIX_W - ATT_WIDTH
LRU_BLOCKS = 16
LRU_BLOCK_W = LRU_WIDTH // LRU_BLOCKS
LRU_CONV = 4
LRU_C = 8.0
ODD_IN = ATT_WIDTH + 2 * ATT_KV_WIDTH + 2 * LRU_WIDTH
N_GROUPS = 4
EXPERTS_PER_GROUP = 8
N_EXPERTS = N_GROUPS * EXPERTS_PER_GROUP
TOP_K = 2
D_EXPERT = 768
MOE_BLOCK = 256

kernel_name = 'hybrid_rwkv7_conformer_gqa_rglru_hmoe_dit'


def rms_norm(x, g, eps=1e-6):
    xf = x.astype(jnp.float32)
    y = xf * lax.rsqrt(jnp.mean(xf * xf, axis=-1, keepdims=True) + eps)
    return (y * g.astype(jnp.float32)).astype(x.dtype)


def layer_norm(x, g, b, eps=1e-5):
    xf = x.astype(jnp.float32)
    mu = jnp.mean(xf, axis=-1, keepdims=True)
    var = jnp.mean(jnp.square(xf - mu), axis=-1, keepdims=True)
    return ((xf - mu) * lax.rsqrt(var + eps) * g + b).astype(x.dtype)


def modulate(h, shift, scale):
    return h * (1 + scale) + shift


def depthwise_conv(x, w, pad):
    return lax.conv_general_dilated(x, w[:, None, :].astype(x.dtype), window_strides=(1,), padding=[pad],
                                    dimension_numbers=('NWC', 'WIO', 'NWC'), feature_group_count=x.shape[-1])


def token_shift(z, reverse):
    if reverse:
        return jnp.pad(z[:, 1:], ((0, 0), (0, 1), (0, 0)))
    return jnp.pad(z[:, :-1], ((0, 0), (1, 0), (0, 0)))


def axial_rope(n_tokens):
    rows = n_tokens // GRID_W
    row = jnp.repeat(jnp.arange(rows, dtype=jnp.float32), GRID_W)
    col = jnp.tile(jnp.arange(GRID_W, dtype=jnp.float32), rows)
    inv = ROPE_THETA ** (-jnp.arange(ROPE_AXIS_PAIRS, dtype=jnp.float32) / ROPE_AXIS_PAIRS)
    ang = jnp.concatenate([row[:, None] * inv, col[:, None] * inv], axis=-1)
    return jnp.cos(ang), jnp.sin(ang)


def apply_rope(x, cos, sin):
    xf = x.astype(jnp.float32)
    half = HEAD_DIM // 2
    x1, x2 = xf[..., :half], xf[..., half:]
    c, s = cos[None, :, None, :], sin[None, :, None, :]
    return jnp.concatenate([x1 * c - x2 * s, x1 * s + x2 * c], axis=-1).astype(x.dtype)


def rwkv7_direction(z, d, reverse, mu, w0, w_up, a0, a_up, k_k, k_a):
    bsz, t, _ = z.shape
    f32 = jnp.float32
    o_w = 3 * RWKV_WIDTH + d * RWKV_LORA_W
    o_a = 3 * RWKV_WIDTH + 2 * RWKV_LORA_W + d * RWKV_LORA_A
    zd = jnp.concatenate([z[..., :3 * RWKV_WIDTH], z[..., o_w:o_w + RWKV_LORA_W],
                          z[..., o_a:o_a + RWKV_LORA_A]], axis=-1)
    zd = zd + (token_shift(zd, reverse) - zd) * mu[d]
    r = zd[..., :RWKV_WIDTH]
    k = zd[..., RWKV_WIDTH:2 * RWKV_WIDTH]
    v = zd[..., 2 * RWKV_WIDTH:3 * RWKV_WIDTH]
    lw = zd[..., 3 * RWKV_WIDTH:3 * RWKV_WIDTH + RWKV_LORA_W]
    la = zd[..., 3 * RWKV_WIDTH + RWKV_LORA_W:]
    w_log = -jax.nn.softplus(-(w0[d] + jnp.tanh(lw) @ w_up[d]).astype(f32)) - 0.5
    decay = jnp.exp(-jnp.exp(w_log))
    a = jax.nn.sigmoid((a0[d] + la @ a_up[d]).astype(f32))
    kf = k.astype(f32)
    heads = lambda u: u.reshape(bsz, t, RWKV_HEADS, HEAD_DIM)
    kk = heads(kf * k_k)
    kk = kk / jnp.maximum(jnp.linalg.norm(kk, axis=-1, keepdims=True), 1e-12)
    kf = kf * (1 + (a - 1) * k_a)
    return heads(r.astype(f32)), heads(decay), heads(kf), heads(v.astype(f32)), kk, heads(a)


def rwkv7_scan(state, r, w, k, v, kk, a, reverse):
    xs = tuple(jnp.moveaxis(u, 1, 0) for u in (r, w, k, v, kk, a))

    def step(s, inp):
        r_t, w_t, k_t, v_t, kk_t, a_t = inp
        sa = jnp.einsum('bhvk,bhk->bhv', s, -kk_t)
        s = s * w_t[:, :, None, :] + sa[..., None] * (kk_t * a_t)[:, :, None, :] + v_t[..., None] * k_t[:, :, None, :]
        return s, jnp.einsum('bhvk,bhk->bhv', s, r_t)

    state, ys = lax.scan(step, state, xs, reverse=reverse)
    return state, jnp.moveaxis(ys, 0, 1)


def rwkv7_output(y, bonus, z, g_up, gn_g, gn_b):
    bsz, t, _ = z.shape
    mean = jnp.mean(y, axis=-1, keepdims=True)
    var = jnp.mean(jnp.square(y - mean), axis=-1, keepdims=True)
    yn = ((y - mean) * lax.rsqrt(var + RWKV_GN_EPS)).reshape(bsz, t, RWKV_WIDTH) * gn_g + gn_b
    yn = yn + bonus.reshape(bsz, t, RWKV_WIDTH)
    g = jax.nn.sigmoid(z[..., RWKV_IN - RWKV_LORA_G:RWKV_IN]) @ g_up
    return yn.astype(z.dtype) * g


def conformer_conv(val, gate, conv_w, conv_b, ln_g, ln_b):
    u = val * jax.nn.sigmoid(gate)
    u = depthwise_conv(u, conv_w, (CONV_KERNEL // 2, CONV_KERNEL // 2)) + conv_b
    return jax.nn.silu(layer_norm(u, ln_g, ln_b))


def even_mixer(h_ctx, h_lat, ctx_out, w_in, mu, w0, w_up, a0, a_up, k_k, k_a, r_k, g_up, gn_g, gn_b,
               conv_w, conv_b, ln_g, ln_b, w_out):
    z = (h_ctx @ w_in, h_lat @ w_in)
    y_sum = [0.0, 0.0]
    bonus = [0.0, 0.0]
    for d, reverse in ((0, False), (1, True)):
        state = jnp.zeros((h_lat.shape[0], RWKV_HEADS, HEAD_DIM, HEAD_DIM), jnp.float32)
        for i in range(2):
            r, w, k, v, kk, a = rwkv7_direction(z[i][..., :RWKV_IN], d, reverse, mu, w0, w_up, a0, a_up, k_k, k_a)
            state, y = rwkv7_scan(state, r, w, k, v, kk, a, reverse)
            y_sum[i] = y_sum[i] + y
            bonus[i] = bonus[i] + jnp.sum(r * k * r_k, axis=-1, keepdims=True) * v

    def merge(i):
        a_out = rwkv7_output(y_sum[i], bonus[i], z[i][..., :RWKV_IN], g_up, gn_g, gn_b)
        zb = z[i][..., RWKV_IN:]
        b_out = conformer_conv(zb[..., :CONV_WIDTH], zb[..., CONV_WIDTH:], conv_w, conv_b, ln_g, ln_b)
        return jnp.concatenate([a_out, b_out], axis=-1) @ w_out

    y_lat = merge(1)
    y_ctx = merge(0) if ctx_out else None
    return y_ctx, y_lat


def gqa_attend(q, k, v):
    s = jnp.einsum('bqkgd,bskd->bkgqs', q, k).astype(jnp.float32) * (HEAD_DIM ** -0.5)
    p = jax.nn.softmax(s, axis=-1).astype(v.dtype)
    return jnp.einsum('bkgqs,bskd->bqkgd', p, v)


def rglru_coeffs(xr, conv_w, conv_b, w_rg, b_rg, w_ig, b_ig, lam, reverse):
    bsz, t, _ = xr.shape
    f32 = jnp.float32
    pad = (0, LRU_CONV - 1) if reverse else (LRU_CONV - 1, 0)
    xc = depthwise_conv(xr, conv_w, pad) + conv_b
    xb = xc.reshape(bsz, t, LRU_BLOCKS, LRU_BLOCK_W)
    rg = jax.nn.sigmoid((jnp.einsum('btnc,ncd->btnd', xb, w_rg) + b_rg).astype(f32)).reshape(bsz, t, LRU_WIDTH)
    ig = jax.nn.sigmoid((jnp.einsum('btnc,ncd->btnd', xb, w_ig) + b_ig).astype(f32)).reshape(bsz, t, LRU_WIDTH)
    log_a = -LRU_C * rg * jax.nn.softplus(-lam.astype(f32))
    a = jnp.exp(log_a)
    b = jnp.sqrt(-jnp.expm1(2.0 * log_a)) * (ig * xc.astype(f32))
    return a, b


def _lin_combine(e1, e2):
    a1, b1 = e1
    a2, b2 = e2
    return a1 * a2, a2 * b1 + b2


def linear_scan(a, b, h0, reverse):
    if h0 is not None:
        idx = a.shape[1] - 1 if reverse else 0
        b = b.at[:, idx].add(a[:, idx] * h0)
    _, h = lax.associative_scan(_lin_combine, (a, b), reverse=reverse, axis=1)
    return h


def odd_mixer(h_ctx, h_lat, cos, sin, ctx_out, w_in, qn_g, kn_g, conv_w, conv_b, w_rg, b_rg, w_ig, b_ig,
              lam, w_out):
    def project(h, rope):
        bsz, t, _ = h.shape
        zz = h @ w_in
        q = rms_norm(zz[..., :ATT_WIDTH].reshape(bsz, t, ATT_HEADS, HEAD_DIM), qn_g)
        k = rms_norm(zz[..., ATT_WIDTH:ATT_WIDTH + ATT_KV_WIDTH].reshape(bsz, t, ATT_KV_HEADS, HEAD_DIM), kn_g)
        v = zz[..., ATT_WIDTH + ATT_KV_WIDTH:ATT_WIDTH + 2 * ATT_KV_WIDTH].reshape(bsz, t, ATT_KV_HEADS, HEAD_DIM)
        if rope:
            q, k = apply_rope(q, cos, sin), apply_rope(k, cos, sin)
        o = ATT_WIDTH + 2 * ATT_KV_WIDTH
        q = q.reshape(bsz, t, ATT_KV_HEADS, ATT_GROUP, HEAD_DIM)
        return q, k, v, zz[..., o:o + LRU_WIDTH], zz[..., o + LRU_WIDTH:]

    qc, kc, vc, rc, gc = project(h_ctx, False)
    ql, kl, vl, rl, gl = project(h_lat, True)
    bsz, s = ql.shape[:2]
    k_all = jnp.concatenate([kc, kl], axis=1)
    v_all = jnp.concatenate([vc, vl], axis=1)
    qb = jnp.moveaxis(ql.reshape(bsz, s // Q_BLOCK, Q_BLOCK, ATT_KV_HEADS, ATT_GROUP, HEAD_DIM), 1, 0)
    att_lat = jnp.moveaxis(lax.map(lambda q: gqa_attend(q, k_all, v_all), qb), 0, 1).reshape(bsz, s, ATT_WIDTH)

    h_sum = [0.0, 0.0]
    for d, reverse in ((0, False), (1, True)):
        h0 = None
        for i, xr in enumerate((rc, rl)):
            a, b = rglru_coeffs(xr, conv_w[d], conv_b[d], w_rg[d], b_rg[d], w_ig[d], b_ig[d], lam[d], reverse)
            hs = linear_scan(a, b, h0, reverse)
            h0 = hs[:, 0] if reverse else hs[:, -1]
            h_sum[i] = h_sum[i] + hs

    def merge(att, hs, gate):
        return jnp.concatenate([att, hs.astype(gate.dtype) * jax.nn.gelu(gate)], axis=-1) @ w_out

    y_lat = merge(att_lat, h_sum[1], gl)
    y_ctx = merge(gqa_attend(qc, kc, vc).reshape(bsz, -1, ATT_WIDTH), h_sum[0], gc) if ctx_out else None
    return y_ctx, y_lat


def hierarchical_moe(h, w_grp, b_grp, w_router, b_router, w_gate, w_up, w_down):
    n_tok, d = h.shape
    f32 = jnp.float32
    grp_prob = jax.nn.softmax((h @ w_grp).astype(f32) + b_grp.astype(f32), axis=-1)
    grp_p, grp_idx = lax.top_k(grp_prob, 1)
    logits = jnp.einsum('nd,gde->nge', h, w_router).astype(f32) + b_router.astype(f32)
    logits = jnp.take_along_axis(logits, grp_idx[:, :, None], axis=1)[:, 0]
    top_p, top_e = lax.top_k(jax.nn.softmax(logits, axis=-1), TOP_K)
    gates = grp_p * top_p / jnp.sum(top_p, axis=-1, keepdims=True)
    expert = grp_idx * EXPERTS_PER_GROUP + top_e
    n_assign = n_tok * TOP_K
    flat_e = expert.reshape(-1)
    order = jnp.argsort(flat_e)
    e_sorted = flat_e[order]
    tok_sorted = jnp.repeat(jnp.arange(n_tok), TOP_K)[order]
    gate_sorted = gates.reshape(-1)[order]
    counts = jnp.zeros((N_EXPERTS,), jnp.int32).at[flat_e].add(1)
    padded = (counts + MOE_BLOCK - 1) // MOE_BLOCK * MOE_BLOCK
    seg_start = jnp.cumsum(counts) - counts
    pad_end = jnp.cumsum(padded)
    slot = (pad_end - padded)[e_sorted] + jnp.arange(n_assign) - seg_start[e_sorted]
    n_blk = -(-n_assign // MOE_BLOCK) + N_EXPERTS
    buf = jnp.zeros((n_blk * MOE_BLOCK, d), h.dtype).at[slot].set(h[tok_sorted])
    blk_expert = jnp.minimum(jnp.searchsorted(pad_end // MOE_BLOCK, jnp.arange(n_blk), side='right'), N_EXPERTS - 1)

    def expert_block(args):
        xb, e = args
        return (jax.nn.silu(xb @ w_gate[e]) * (xb @ w_up[e])) @ w_down[e]

    out = lax.map(expert_block, (buf.reshape(n_blk, MOE_BLOCK, d), blk_expert)).reshape(-1, d)
    return jnp.zeros_like(h).at[tok_sorted].add(out[slot] * gate_sorted[:, None].astype(h.dtype))


def setup_inputs(seed: int = 0) -> dict:
    key = jax.random.key(seed)
    keys = jax.random.split(key, 64)
    counter = [0]
    f32 = jnp.float32

    def nk():
        counter[0] += 1
        return keys[counter[0] - 1]

    def nrm(shape, scale):
        return jax.random.normal(nk(), shape, f32) * scale

    def gain(shape):
        return 1.0 + nrm(shape, 0.02)

    def unif(shape, lo, hi):
        return jax.random.uniform(nk(), shape, f32, lo, hi)

    D = D_MODEL
    lru_p = unif((N_ODD, 2, LRU_WIDTH), 0.9, 0.999) ** (1.0 / LRU_C)
    return {
        'x': nrm((BATCH, SEQ, D), 1.0),
        'c': nrm((BATCH, D), 1.0),
        'ctx': nrm((BATCH, CTX_LEN, D), 1.0),
        'c_ctx': nrm((D,), 1.0),
        'norm_mix_g': gain((DEPTH, D)),
        'norm_ffn_g': gain((DEPTH, D)),
        'ada_w': nrm((DEPTH, D, MOD_CHUNKS * D), 0.5 * D ** -0.5),
        'ada_b': nrm((DEPTH, MOD_CHUNKS * D), 0.02),
        'even_w_in': nrm((N_EVEN, D, EVEN_IN), D ** -0.5),
        'rwkv_mu': unif((N_EVEN, 2, RWKV_SHIFT_W), 0.0, 1.0),
        'rwkv_w0': unif((N_EVEN, 2, RWKV_WIDTH), -6.0, -1.0),
        'rwkv_w_up': nrm((N_EVEN, 2, RWKV_LORA_W, RWKV_WIDTH), 0.1),
        'rwkv_a0': nrm((N_EVEN, 2, RWKV_WIDTH), 0.5),
        'rwkv_a_up': nrm((N_EVEN, 2, RWKV_LORA_A, RWKV_WIDTH), 0.5 * RWKV_LORA_A ** -0.5),
        'rwkv_k_k': 0.85 + nrm((N_EVEN, RWKV_WIDTH), 0.02),
        'rwkv_k_a': gain((N_EVEN, RWKV_WIDTH)),
        'rwkv_r_k': nrm((N_EVEN, RWKV_HEADS, HEAD_DIM), 0.1),
        'rwkv_g_up': nrm((N_EVEN, RWKV_LORA_G, RWKV_WIDTH), RWKV_LORA_G ** -0.5),
        'rwkv_gn_g': gain((N_EVEN, RWKV_WIDTH)),
        'rwkv_gn_b': nrm((N_EVEN, RWKV_WIDTH), 0.02),
        'conv_w': nrm((N_EVEN, CONV_KERNEL, CONV_WIDTH), CONV_KERNEL ** -0.5),
        'conv_b': nrm((N_EVEN, CONV_WIDTH), 0.02),
        'conv_ln_g': gain((N_EVEN, CONV_WIDTH)),
        'conv_ln_b': nrm((N_EVEN, CONV_WIDTH), 0.02),
        'even_w_out': nrm((N_EVEN, MIX_W, D), MIX_W ** -0.5),
        'odd_w_in': nrm((N_ODD, D, ODD_IN), D ** -0.5),
        'att_qn_g': gain((N_ODD, HEAD_DIM)),
        'att_kn_g': gain((N_ODD, HEAD_DIM)),
        'lru_conv_w': nrm((N_ODD, 2, LRU_CONV, LRU_WIDTH), LRU_CONV ** -0.5),
        'lru_conv_b': nrm((N_ODD, 2, LRU_WIDTH), 0.02),
        'lru_w_rg': nrm((N_ODD, 2, LRU_BLOCKS, LRU_BLOCK_W, LRU_BLOCK_W), LRU_BLOCK_W ** -0.5),
        'lru_b_rg': nrm((N_ODD, 2, LRU_BLOCKS, LRU_BLOCK_W), 0.02),
        'lru_w_ig': nrm((N_ODD, 2, LRU_BLOCKS, LRU_BLOCK_W, LRU_BLOCK_W), LRU_BLOCK_W ** -0.5),
        'lru_b_ig': nrm((N_ODD, 2, LRU_BLOCKS, LRU_BLOCK_W), 0.02),
        'lru_lambda': jnp.log(lru_p) - jnp.log1p(-lru_p),
        'odd_w_out': nrm((N_ODD, MIX_W, D), MIX_W ** -0.5),
        'moe_w_grp': nrm((DEPTH, D, N_GROUPS), D ** -0.5),
        'moe_b_grp': nrm((DEPTH, N_GROUPS), 0.01),
        'moe_w_router': nrm((DEPTH, N_GROUPS, D, EXPERTS_PER_GROUP), D ** -0.5),
        'moe_b_router': nrm((DEPTH, N_GROUPS, EXPERTS_PER_GROUP), 0.01),
        'moe_w_gate': nrm((DEPTH, N_EXPERTS, D, D_EXPERT), D ** -0.5),
        'moe_w_up': nrm((DEPTH, N_EXPERTS, D, D_EXPERT), D ** -0.5),
        'moe_w_down': nrm((DEPTH, N_EXPERTS, D_EXPERT, D), D_EXPERT ** -0.5),
        'final_g': gain((D,)),
    }


def reference(x, c, ctx, c_ctx, norm_mix_g, norm_ffn_g, ada_w, ada_b, even_w_in, rwkv_mu, rwkv_w0, rwkv_w_up,
              rwkv_a0, rwkv_a_up, rwkv_k_k, rwkv_k_a, rwkv_r_k, rwkv_g_up, rwkv_gn_g, rwkv_gn_b, conv_w, conv_b,
              conv_ln_g, conv_ln_b, even_w_out, odd_w_in, att_qn_g, att_kn_g, lru_conv_w, lru_conv_b, lru_w_rg,
              lru_b_rg, lru_w_ig, lru_b_ig, lru_lambda, odd_w_out, moe_w_grp, moe_b_grp, moe_w_router,
              moe_b_router, moe_w_gate, moe_w_up, moe_w_down, final_g):
    bsz, s, d_model = x.shape
    n_ctx = ctx.shape[1]
    cos, sin = axial_rope(s)
    silu_c = jax.nn.silu(c)
    silu_cc = jax.nn.silu(c_ctx)
    xc = ctx
    for layer in range(DEPTH):
        ctx_out = layer < DEPTH - 1
        j = layer // 2
        mod = silu_c @ ada_w[layer] + ada_b[layer]
        mod_c = silu_cc @ ada_w[layer] + ada_b[layer]
        sh1, sc1, g1, sh2, sc2, g2 = jnp.split(mod[:, None, :], MOD_CHUNKS, axis=-1)
        csh1, csc1, cg1, csh2, csc2, cg2 = jnp.split(mod_c, MOD_CHUNKS)
        h_lat = modulate(rms_norm(x, norm_mix_g[layer]), sh1, sc1)
        h_ctx = modulate(rms_norm(xc, norm_mix_g[layer]), csh1, csc1)
        if layer % 2 == 0:
            y_ctx, y_lat = even_mixer(h_ctx, h_lat, ctx_out, even_w_in[j], rwkv_mu[j], rwkv_w0[j], rwkv_w_up[j],
                                      rwkv_a0[j], rwkv_a_up[j], rwkv_k_k[j], rwkv_k_a[j], rwkv_r_k[j],
                                      rwkv_g_up[j], rwkv_gn_g[j], rwkv_gn_b[j], conv_w[j], conv_b[j],
                                      conv_ln_g[j], conv_ln_b[j], even_w_out[j])
        else:
            y_ctx, y_lat = odd_mixer(h_ctx, h_lat, cos, sin, ctx_out, odd_w_in[j], att_qn_g[j], att_kn_g[j],
                                     lru_conv_w[j], lru_conv_b[j], lru_w_rg[j], lru_b_rg[j], lru_w_ig[j],
                                     lru_b_ig[j], lru_lambda[j], odd_w_out[j])
        x = x + g1 * y_lat
        moe_p = (moe_w_grp[layer], moe_b_grp[layer], moe_w_router[layer], moe_b_router[layer],
                 moe_w_gate[layer], moe_w_up[layer], moe_w_down[layer])
        f_lat = modulate(rms_norm(x, norm_ffn_g[layer]), sh2, sc2).reshape(bsz * s, d_model)
        if ctx_out:
            xc = xc + cg1 * y_ctx
            f_ctx = modulate(rms_norm(xc, norm_ffn_g[layer]), csh2, csc2).reshape(bsz * n_ctx, d_model)
            f = hierarchical_moe(jnp.concatenate([f_lat, f_ctx], axis=0), *moe_p)
            x = x + g2 * f[:bsz * s].reshape(bsz, s, d_model)
            xc = xc + cg2 * f[bsz * s:].reshape(bsz, n_ctx, d_model)
        else:
            x = x + g2 * hierarchical_moe(f_lat, *moe_p).reshape(bsz, s, d_model)
    return rms_norm(x, final_g)
```

```python
import functools
import math

import jax
import jax.numpy as jnp
from jax import lax
from jax.experimental import pallas as pl
from jax.experimental.pallas import tpu as pltpu

F32 = jnp.float32
BF16 = jnp.bfloat16

GRID_W = 64
ROPE_THETA = 10000.0
RWKV_GN_EPS = 64e-5
LRU_C = 8.0
TOP_K = 2
RMS_EPS = 1e-6
LN_EPS = 1e-5
LANES = 128
MXU_DIM = 256
VMEM_LIMIT = 48 * 2**20
MOE_ROWS = 256
SCAN_TB = 32


def _params(sem):
    return pltpu.CompilerParams(dimension_semantics=sem, vmem_limit_bytes=VMEM_LIMIT)


def _dot(a, b):
    return jnp.dot(a, b, preferred_element_type=F32)


def _split(w):
    hi = w.astype(BF16)
    return hi, (w - hi.astype(F32)).astype(BF16)


def _dot3(a, w_hi, w_lo):
    a_hi = a.astype(BF16)
    a_lo = (a - a_hi.astype(F32)).astype(BF16)
    return _dot(a_hi, w_hi) + _dot(a_lo, w_hi) + _dot(a_hi, w_lo)


def _sigmoid(x):
    return 1.0 / (1.0 + jnp.exp(-x))


def _pow2_tile(limit, *dims):
    t = limit
    while any(d % t for d in dims):
        t //= 2
    return t


class _Dims:
    def __init__(self, b, s, c, d):
        self.b, self.s, self.c, self.d = b, s, c, d
        self.n_lat = b * s
        self.n = b * s + b * c
        self.tm = _pow2_tile(1024, s, b * c)
        self.tr = _pow2_tile(256, s, c)
        assert self.tr == c, "sequence kernels assume one context tile per batch element"

    def mod_index(self, tile):
        n_lat_tiles = self.n_lat // tile

        def f(i):
            return jnp.where(i < n_lat_tiles, (i * tile) // self.s, self.b)
        return f


def _ada_kernel(a_ref, wh_ref, wl_ref, b_ref, o_ref):
    o_ref[...] = _dot3(a_ref[...], wh_ref[...], wl_ref[...]) + b_ref[...]


def ada_matmul(a, w, bias):
    r, d = a.shape
    m = w.shape[1]
    tn = _pow2_tile(1024, m)
    wh, wl = _split(w)
    return pl.pallas_call(
        _ada_kernel,
        grid=(m // tn,),
        in_specs=[pl.BlockSpec((r, d), lambda j: (0, 0)),
                  pl.BlockSpec((d, tn), lambda j: (0, j)),
                  pl.BlockSpec((d, tn), lambda j: (0, j)),
                  pl.BlockSpec((1, tn), lambda j: (0, j))],
        out_specs=pl.BlockSpec((r, tn), lambda j: (0, j)),
        out_shape=jax.ShapeDtypeStruct((r, m), F32),
        compiler_params=_params(("arbitrary",)),
        name="ada_matmul",
    )(a, wh, wl, bias.reshape(1, m))


def _norm_mod(x_ref, g_ref, sh_ref, sc_ref):
    xf = x_ref[...]
    ms = jnp.mean(xf * xf, axis=-1, keepdims=True)
    y = xf * lax.rsqrt(ms + RMS_EPS) * g_ref[...]
    return y * (1.0 + sc_ref[0]) + sh_ref[0]


def _nm_matmul_kernel(x_ref, g_ref, sh_ref, sc_ref, w_ref, z_ref, h_scr):
    @pl.when(pl.program_id(1) == 0)
    def _():
        h_scr[...] = _norm_mod(x_ref, g_ref, sh_ref, sc_ref).astype(BF16)

    z_ref[...] = _dot(h_scr[...], w_ref[...])


def nm_matmul(dims, x, g, shift, scale, w, n_rows):
    d = dims.d
    m = w.shape[1]
    tm = dims.tm
    tn = _pow2_tile(512, m)
    midx = dims.mod_index(tm)
    return pl.pallas_call(
        _nm_matmul_kernel,
        grid=(n_rows // tm, m // tn),
        in_specs=[pl.BlockSpec((tm, d), lambda i, j: (i, 0)),
                  pl.BlockSpec((1, d), lambda i, j: (0, 0)),
                  pl.BlockSpec((1, 1, d), lambda i, j: (midx(i), 0, 0)),
                  pl.BlockSpec((1, 1, d), lambda i, j: (midx(i), 0, 0)),
                  pl.BlockSpec((d, tn), lambda i, j: (0, j))],
        out_specs=pl.BlockSpec((tm, tn), lambda i, j: (i, j)),
        out_shape=jax.ShapeDtypeStruct((n_rows, m), F32),
        scratch_shapes=[pltpu.VMEM((tm, d), BF16)],
        compiler_params=_params(("parallel", "arbitrary")),
        name="nm_matmul",
    )(x, g.reshape(1, d), shift, scale, w)


def _nm_router_kernel(x_ref, g_ref, sh_ref, sc_ref, wh_ref, wl_ref, b_ref, f_ref, lg_ref):
    h = _norm_mod(x_ref, g_ref, sh_ref, sc_ref)
    f_ref[...] = h.astype(BF16)
    lg_ref[...] = _dot3(h, wh_ref[...], wl_ref[...]) + b_ref[...]


def nm_router(dims, x, g, shift, scale, w_rt, b_rt, n_rows):
    d = dims.d
    tm = dims.tm
    m = w_rt.shape[1]
    wh, wl = _split(w_rt)
    midx = dims.mod_index(tm)
    return pl.pallas_call(
        _nm_router_kernel,
        grid=(n_rows // tm,),
        in_specs=[pl.BlockSpec((tm, d), lambda i: (i, 0)),
                  pl.BlockSpec((1, d), lambda i: (0, 0)),
                  pl.BlockSpec((1, 1, d), lambda i: (midx(i), 0, 0)),
                  pl.BlockSpec((1, 1, d), lambda i: (midx(i), 0, 0)),
                  pl.BlockSpec((d, m), lambda i: (0, 0)),
                  pl.BlockSpec((d, m), lambda i: (0, 0)),
                  pl.BlockSpec((1, m), lambda i: (0, 0))],
        out_specs=[pl.BlockSpec((tm, d), lambda i: (i, 0)),
                   pl.BlockSpec((tm, m), lambda i: (i, 0))],
        out_shape=[jax.ShapeDtypeStruct((n_rows, d), BF16),
                   jax.ShapeDtypeStruct((n_rows, m), F32)],
        compiler_params=_params(("parallel",)),
        name="nm_router",
    )(x, g.reshape(1, d), shift, scale, wh, wl, b_rt.reshape(1, m))


def _mm_res_kernel(a1_ref, a2_ref, w1_ref, w2_ref, x_ref, gt_ref, o_ref):
    acc = _dot(a1_ref[...], w1_ref[...]) + _dot(a2_ref[...], w2_ref[...])
    o_ref[...] = x_ref[...] + gt_ref[0] * acc


def matmul_residual(dims, a1, a2, w1, w2, x, gate, n_rows):
    d = dims.d
    tm = dims.tm
    tn = _pow2_tile(512, d)
    k1, k2 = a1.shape[1], a2.shape[1]
    midx = dims.mod_index(tm)
    out = pl.pallas_call(
        _mm_res_kernel,
        grid=(n_rows // tm, d // tn),
        in_specs=[pl.BlockSpec((tm, k1), lambda i, j: (i, 0)),
                  pl.BlockSpec((tm, k2), lambda i, j: (i, 0)),
                  pl.BlockSpec((k1, tn), lambda i, j: (0, j)),
                  pl.BlockSpec((k2, tn), lambda i, j: (0, j)),
                  pl.BlockSpec((tm, tn), lambda i, j: (i, j)),
                  pl.BlockSpec((1, 1, tn), lambda i, j: (midx(i), 0, j))],
        out_specs=pl.BlockSpec((tm, tn), lambda i, j: (i, j)),
        out_shape=jax.ShapeDtypeStruct((n_rows, d), F32),
        compiler_params=_params(("parallel", "arbitrary")),
        name="matmul_residual",
    )(a1, a2, w1, w2, x, gate)
    return out


def _combine_kernel(x_ref, y0_ref, y1_ref, gt_ref, o_ref):
    o_ref[...] = x_ref[...] + gt_ref[0] * (y0_ref[...] + y1_ref[...])


def combine_residual(dims, x, y0, y1, gate, n_rows):
    d = dims.d
    tm = _pow2_tile(512, dims.tm)
    midx = dims.mod_index(tm)
    row = pl.BlockSpec((tm, d), lambda i: (i, 0))
    return pl.pallas_call(
        _combine_kernel,
        grid=(n_rows // tm,),
        in_specs=[row, row, row, pl.BlockSpec((1, 1, d), lambda i: (midx(i), 0, 0))],
        out_specs=row,
        out_shape=jax.ShapeDtypeStruct((n_rows, d), F32),
        compiler_params=_params(("parallel",)),
        name="combine_residual",
    )(x, y0, y1, gate)


def _final_norm_kernel(x_ref, g_ref, o_ref):
    xf = x_ref[...]
    ms = jnp.mean(xf * xf, axis=-1, keepdims=True)
    o_ref[...] = xf * lax.rsqrt(ms + RMS_EPS) * g_ref[...]


def final_norm(dims, x, g, n_rows):
    d = dims.d
    tm = _pow2_tile(512, dims.tm)
    return pl.pallas_call(
        _final_norm_kernel,
        grid=(n_rows // tm,),
        in_specs=[pl.BlockSpec((tm, d), lambda i: (i, 0)), pl.BlockSpec((1, d), lambda i: (0, 0))],
        out_specs=pl.BlockSpec((tm, d), lambda i: (i, 0)),
        out_shape=jax.ShapeDtypeStruct((n_rows, d), F32),
        compiler_params=_params(("parallel",)),
        name="final_norm",
    )(x, g.reshape(1, d))


def _seg_edges(dims, i):
    tr = dims.tr
    n_lat_tiles = dims.n_lat // tr
    per_seq = dims.s // tr
    is_lat = i < n_lat_tiles
    first = jnp.where(is_lat, i % per_seq == 0, True)
    last = jnp.where(is_lat, i % per_seq == per_seq - 1, True)
    return first, last


def _block_ones(hd):
    idx = jnp.arange(LANES) // hd
    return (idx[:, None] == idx[None, :]).astype(BF16)


def _seg_sum(x, q):
    outs = []
    for c in range(x.shape[1] // LANES):
        xc = x[:, c * LANES:(c + 1) * LANES]
        hi = xc.astype(BF16)
        lo = (xc - hi.astype(F32)).astype(BF16)
        outs.append(_dot(hi, q) + _dot(lo, q))
    return jnp.concatenate(outs, axis=1) if len(outs) > 1 else outs[0]


def _neighbour(x, halo_ref, reverse, at_edge):
    rows = x.shape[0]
    rid = lax.broadcasted_iota(jnp.int32, (rows, 1), 0)
    if reverse:
        edge_row = jnp.where(at_edge, 0.0, halo_ref[0:1, :])
        return jnp.where(rid == rows - 1, edge_row, pltpu.roll(x, rows - 1, axis=0))
    edge_row = jnp.where(at_edge, 0.0, halo_ref[halo_ref.shape[0] - 1:, :])
    return jnp.where(rid == 0, edge_row, pltpu.roll(x, 1, axis=0))


def _rwkv_prep_kernel(r_ref, k_ref, v_ref, lw_ref, la_ref, rh_ref, kh_ref, vh_ref, lwh_ref, lah_ref,
                      mur_ref, muk_ref, muv_ref, mulw_ref, mula_ref, w0_ref, wuh_ref, wul_ref,
                      a0_ref, auh_ref, aul_ref, kk_ref, ka_ref, rk_ref, q_ref,
                      ro_ref, wo_ref, ko_ref, vo_ref, kko_ref, kao_ref, bo_ref, *, dims, reverse):
    first, last = _seg_edges(dims, pl.program_id(0))
    at_edge = last if reverse else first

    def mix(x_ref, h_ref, mu_ref):
        x = x_ref[...]
        return x + (_neighbour(x, h_ref, reverse, at_edge) - x) * mu_ref[...]

    r = mix(r_ref, rh_ref, mur_ref)
    k = mix(k_ref, kh_ref, muk_ref)
    v = mix(v_ref, vh_ref, muv_ref)
    lw = mix(lw_ref, lwh_ref, mulw_ref)
    la = mix(la_ref, lah_ref, mula_ref)
    q = q_ref[...]
    u = w0_ref[...] + _dot3(jnp.tanh(lw), wuh_ref[...], wul_ref[...])
    decay = jnp.exp(-math.exp(-0.5) * _sigmoid(u))
    a = _sigmoid(a0_ref[...] + _dot3(la, auh_ref[...], aul_ref[...]))
    kk = k * kk_ref[...]
    kk = kk * lax.rsqrt(jnp.maximum(_seg_sum(kk * kk, q), 1e-24))
    kf = k * (1.0 + (a - 1.0) * ka_ref[...])
    ro_ref[...] = r
    wo_ref[...] = decay
    ko_ref[...] = kf
    vo_ref[...] = v
    kko_ref[...] = kk
    kao_ref[...] = kk * a
    bo_ref[...] = _seg_sum(r * kf * rk_ref[...], q) * v


def rwkv_prep(dims, z_rkv, z_lora, p, d_idx, hd):
    reverse = d_idx == 1
    tr, n = dims.tr, dims.n
    w = p["w0"].shape[-1]
    lw_w = p["w_up"].shape[-2]
    la_w = p["a_up"].shape[-2]
    assert 2 * lw_w == LANES and 2 * la_w == LANES
    mu = p["mu"][d_idx]

    def half(vec, width):
        z = jnp.zeros_like(vec)
        return jnp.concatenate([vec, z] if d_idx == 0 else [z, vec], axis=0)

    mu_lw = half(mu[3 * w:3 * w + lw_w], lw_w).reshape(1, LANES)
    mu_la = half(mu[3 * w + lw_w:], la_w).reshape(1, LANES)
    wuh, wul = _split(half(p["w_up"][d_idx], lw_w))
    auh, aul = _split(half(p["a_up"][d_idx], la_w))
    row = lambda c: pl.BlockSpec((tr, w), lambda i: (i, c))
    chunk = lambda c: pl.BlockSpec((tr, LANES), lambda i: (i, c))

    def halo_c(c):
        per = tr // 8
        nb = n // 8
        if reverse:
            return pl.BlockSpec((8, LANES), lambda i: (jnp.minimum((i + 1) * per, nb - 1), c))
        return pl.BlockSpec((8, LANES), lambda i: (jnp.maximum(i * per - 1, 0), c))

    def halo_w(c):
        per = tr // 8
        nb = n // 8
        if reverse:
            return pl.BlockSpec((8, w), lambda i: (jnp.minimum((i + 1) * per, nb - 1), c))
        return pl.BlockSpec((8, w), lambda i: (jnp.maximum(i * per - 1, 0), c))

    vec = lambda width: pl.BlockSpec((1, width), lambda i: (0, 0))
    mat = lambda: pl.BlockSpec((LANES, w), lambda i: (0, 0))
    outs = pl.pallas_call(
        functools.partial(_rwkv_prep_kernel, dims=dims, reverse=reverse),
        grid=(n // tr,),
        in_specs=[row(0), row(1), row(2), chunk(0), chunk(1),
                  halo_w(0), halo_w(1), halo_w(2), halo_c(0), halo_c(1),
                  vec(w), vec(w), vec(w), vec(LANES), vec(LANES), vec(w), mat(), mat(),
                  vec(w), mat(), mat(), vec(w), vec(w), vec(w),
                  pl.BlockSpec((LANES, LANES), lambda i: (0, 0))],
        out_specs=[pl.BlockSpec((tr, w), lambda i: (i, 0))] * 7,
        out_shape=[jax.ShapeDtypeStruct((n, w), F32)] * 7,
        compiler_params=_params(("parallel",)),
        name="rwkv_prep",
    )(z_rkv, z_rkv, z_rkv, z_lora, z_lora, z_rkv, z_rkv, z_rkv, z_lora, z_lora,
      mu[:w].reshape(1, w), mu[w:2 * w].reshape(1, w), mu[2 * w:3 * w].reshape(1, w), mu_lw, mu_la,
      p["w0"][d_idx].reshape(1, w), wuh, wul, p["a0"][d_idx].reshape(1, w), auh, aul,
      p["k_k"].reshape(1, w), p["k_a"].reshape(1, w), p["r_k"].reshape(1, w), _block_ones(hd))
    return outs


def _rwkv_scan_kernel(r_ref, w_ref, k_ref, v_ref, kk_ref, ka_ref, y_ref, s_ref, wr_ref, *, tb, hd):
    d = pl.program_id(0)

    @pl.when(pl.program_id(1) == 0)
    def _():
        s_ref[...] = jnp.zeros_like(s_ref)

    def step(s, carry):
        t = jnp.where(d == 0, s, tb - 1 - s)
        r_t = r_ref[0, t]
        v_t = v_ref[0, t]
        wr_ref[...] = w_ref[0, t] * r_t
        c1 = jnp.sum(ka_ref[0, t] * r_t, axis=0, keepdims=True)
        c2 = jnp.sum(k_ref[0, t] * r_t, axis=0, keepdims=True)
        acc_sa = jnp.zeros_like(v_t)
        acc_u = jnp.zeros_like(v_t)
        for kx in range(hd):
            sk = s_ref[kx]
            acc_sa = acc_sa + sk * kk_ref[0, t, pl.ds(kx, 1), :]
            acc_u = acc_u + sk * wr_ref[pl.ds(kx, 1), :]
        sa = -acc_sa
        y_ref[0, t] = acc_u + sa * c1 + v_t * c2
        for kx in range(hd):
            s_ref[kx] = (s_ref[kx] * w_ref[0, t, pl.ds(kx, 1), :] + sa * ka_ref[0, t, pl.ds(kx, 1), :]
                         + v_t * k_ref[0, t, pl.ds(kx, 1), :])
        return carry

    lax.fori_loop(0, tb, step, 0)


def rwkv_scan(xs, n_ctx):
    _, t_all, hd, lanes = xs[0].shape
    tb = _pow2_tile(SCAN_TB, n_ctx, t_all - n_ctx)
    n_blocks = t_all // tb
    n_ctx_blocks = n_ctx // tb

    def tmap(d, i):
        rev = jnp.where(i < n_ctx_blocks, n_ctx_blocks - 1 - i, n_blocks - 1 - (i - n_ctx_blocks))
        return (d, jnp.where(d == 0, i, rev), 0, 0)

    spec = pl.BlockSpec((1, tb, hd, lanes), tmap)
    return pl.pallas_call(
        functools.partial(_rwkv_scan_kernel, tb=tb, hd=hd),
        grid=(2, n_blocks),
        in_specs=[spec] * 6,
        out_specs=spec,
        out_shape=jax.ShapeDtypeStruct(xs[0].shape, F32),
        scratch_shapes=[pltpu.VMEM((hd, hd, lanes), F32), pltpu.VMEM((hd, lanes), F32)],
        compiler_params=_params(("arbitrary", "arbitrary")),
        name="rwkv_scan",
    )(*xs)


def _rwkv_gn_kernel(y_ref, g_ref, b_ref, o_ref):
    y = y_ref[0] + y_ref[1]
    mean = jnp.mean(y, axis=1, keepdims=True)
    yc = y - mean
    var = jnp.mean(yc * yc, axis=1, keepdims=True)
    o_ref[...] = yc * lax.rsqrt(var + RWKV_GN_EPS) * g_ref[...] + b_ref[...]


def rwkv_group_norm(y, g_tab, b_tab):
    _, t_all, hd, lanes = y.shape
    tb = _pow2_tile(64, t_all)
    return pl.pallas_call(
        _rwkv_gn_kernel,
        grid=(t_all // tb,),
        in_specs=[pl.BlockSpec((2, tb, hd, lanes), lambda i: (0, i, 0, 0)),
                  pl.BlockSpec((1, hd, lanes), lambda i: (0, 0, 0)),
                  pl.BlockSpec((1, hd, lanes), lambda i: (0, 0, 0))],
        out_specs=pl.BlockSpec((tb, hd, lanes), lambda i: (i, 0, 0)),
        out_shape=jax.ShapeDtypeStruct((t_all, hd, lanes), F32),
        compiler_params=_params(("parallel",)),
        name="rwkv_group_norm",
    )(y, g_tab, b_tab)


def _even_merge_kernel(yn_ref, b0_ref, b1_ref, zg_ref, val_ref, gate_ref, valp_ref, gatep_ref, valn_ref,
                       gaten_ref, guh_ref, gul_ref, cw_ref, cb_ref, lng_ref, lnb_ref,
                       a_ref, bo_ref, u_scr, *, dims, taps, halo):
    first, last = _seg_edges(dims, pl.program_id(0))
    tr = dims.tr
    g = _dot3(_sigmoid(zg_ref[...]), guh_ref[...], gul_ref[...])
    a_ref[...] = ((yn_ref[...] + b0_ref[...] + b1_ref[...]) * g).astype(BF16)

    glu = lambda v_ref, g_ref: v_ref[...] * _sigmoid(g_ref[...])
    u_scr[0:halo, :] = jnp.where(first, 0.0, glu(valp_ref, gatep_ref))
    u_scr[halo:halo + tr, :] = glu(val_ref, gate_ref)
    u_scr[halo + tr:, :] = jnp.where(last, 0.0, glu(valn_ref, gaten_ref))
    pad = taps // 2
    acc = jnp.zeros((tr, u_scr.shape[1]), F32) + cb_ref[...]
    for j in range(taps):
        acc = acc + cw_ref[j:j + 1, :] * u_scr[pl.ds(halo - pad + j, tr), :]
    mu = jnp.mean(acc, axis=-1, keepdims=True)
    xc = acc - mu
    var = jnp.mean(xc * xc, axis=-1, keepdims=True)
    y = xc * lax.rsqrt(var + LN_EPS) * lng_ref[...] + lnb_ref[...]
    bo_ref[...] = (y * _sigmoid(y)).astype(BF16)


def even_merge(dims, yn, bonus0, bonus1, z_lora, z_cv, p):
    tr, n = dims.tr, dims.n
    w = yn.shape[1]
    taps, cw = p["conv_w"].shape
    lg = p["g_up"].shape[0]
    assert lg == LANES
    halo = 16
    assert taps // 2 <= halo and tr % halo == 0
    guh, gul = _split(p["g_up"])
    per = tr // halo
    nb = n // halo
    row_w = pl.BlockSpec((tr, w), lambda i: (i, 0))
    col = lambda c: pl.BlockSpec((tr, cw), lambda i: (i, c))
    prev = lambda c: pl.BlockSpec((halo, cw), lambda i: (jnp.maximum(i * per - 1, 0), c))
    nxt = lambda c: pl.BlockSpec((halo, cw), lambda i: (jnp.minimum((i + 1) * per, nb - 1), c))
    vec = lambda width: pl.BlockSpec((1, width), lambda i: (0, 0))
    return pl.pallas_call(
        functools.partial(_even_merge_kernel, dims=dims, taps=taps, halo=halo),
        grid=(n // tr,),
        in_specs=[row_w, row_w, row_w, pl.BlockSpec((tr, LANES), lambda i: (i, 2)),
                  col(0), col(1), prev(0), prev(1), nxt(0), nxt(1),
                  pl.BlockSpec((lg, w), lambda i: (0, 0)), pl.BlockSpec((lg, w), lambda i: (0, 0)),
                  pl.BlockSpec((taps, cw), lambda i: (0, 0)), vec(cw), vec(cw), vec(cw)],
        out_specs=[pl.BlockSpec((tr, w), lambda i: (i, 0)), pl.BlockSpec((tr, cw), lambda i: (i, 0))],
        out_shape=[jax.ShapeDtypeStruct((n, w), BF16), jax.ShapeDtypeStruct((n, cw), BF16)],
        scratch_shapes=[pltpu.VMEM((tr + 2 * halo, cw), F32)],
        compiler_params=_params(("parallel",)),
        name="even_merge",
    )(yn, bonus0, bonus1, z_lora, z_cv, z_cv, z_cv, z_cv, z_cv, z_cv, guh, gul,
      p["conv_w"], p["conv_b"].reshape(1, cw), p["ln_g"].reshape(1, cw), p["ln_b"].reshape(1, cw))


def even_mixer(dims, x, g_norm, mod, p, hd, n_out_rows):
    b, s, c, n = dims.b, dims.s, dims.c, dims.n
    w = p["w0"].shape[-1]
    cw = p["conv_w"].shape[1]
    lw_w, la_w, lg = p["w_up"].shape[-2], p["a_up"].shape[-2], p["g_up"].shape[0]
    w_in = p["w_in"]
    o = 3 * w
    lora_cols = w_in[:, o:o + 2 * lw_w + 2 * la_w + lg]
    lora_pad = jnp.pad(lora_cols, ((0, 0), (0, -lora_cols.shape[1] % (4 * LANES))))
    rwkv_in = o + 2 * lw_w + 2 * la_w + lg
    z_rkv = nm_matmul(dims, x, g_norm, mod["sh1"], mod["sc1"], w_in[:, :o].astype(BF16), n)
    z_lora = nm_matmul(dims, x, g_norm, mod["sh1"], mod["sc1"], lora_pad.astype(BF16), n)
    z_cv = nm_matmul(dims, x, g_norm, mod["sh1"], mod["sc1"], w_in[:, rwkv_in:].astype(BF16), n)

    heads = w // hd

    def to_scan(a):
        lat = a[:dims.n_lat].reshape(b, s, heads, hd)
        ctx = a[dims.n_lat:].reshape(b, c, heads, hd)
        al = jnp.concatenate([ctx, lat], axis=1)
        return al.transpose(1, 3, 0, 2).reshape(c + s, hd, b * heads)

    per_dir = [rwkv_prep(dims, z_rkv, z_lora, p, d_idx, hd) for d_idx in range(2)]
    xs = [jnp.stack([to_scan(per_dir[0][q]), to_scan(per_dir[1][q])]) for q in range(6)]
    y = rwkv_scan(xs, c)

    def head_tab(vec):
        return jnp.tile(vec.reshape(heads, hd).T[:, None, :], (1, b, 1)).reshape(1, hd, b * heads)

    yn = rwkv_group_norm(y, head_tab(p["gn_g"]), head_tab(p["gn_b"]))
    yn = yn.reshape(c + s, hd, b, heads).transpose(2, 0, 3, 1)
    yn = jnp.concatenate([yn[:, c:].reshape(b * s, w), yn[:, :c].reshape(b * c, w)], axis=0)
    a_out, b_out = even_merge(dims, yn, per_dir[0][6], per_dir[1][6], z_lora, z_cv, p)
    w_out = p["w_out"].astype(BF16)
    return matmul_residual(dims, a_out, b_out, w_out[:w], w_out[w:], x, mod["g1"], n_out_rows)


def _qkv_prep_kernel(zq_ref, zk_ref, zv_ref, cos_ref, sin_ref, gq_ref, gk_ref, qm_ref,
                     q_ref, k_ref, v_ref, *, hd):
    qm = qm_ref[...]
    cos = cos_ref[...]
    sin = sin_ref[...]
    lane = lax.broadcasted_iota(jnp.int32, (1, LANES), 1)
    low = (lane % hd) < hd // 2

    def norm_rope(z, gain, scale):
        ms = _seg_sum(z * z, qm) * (1.0 / hd)
        y = z * lax.rsqrt(ms + RMS_EPS) * gain
        outs = []
        for c in range(y.shape[1] // LANES):
            yc = y[:, c * LANES:(c + 1) * LANES]
            partner = jnp.where(low, pltpu.roll(yc, LANES - hd // 2, axis=1), pltpu.roll(yc, hd // 2, axis=1))
            outs.append((yc * cos + partner * sin) * scale)
        return jnp.concatenate(outs, axis=1) if len(outs) > 1 else outs[0]

    q_ref[...] = norm_rope(zq_ref[...], gq_ref[...], hd ** -0.5).astype(BF16)
    k_ref[...] = norm_rope(zk_ref[...], gk_ref[...], 1.0).astype(BF16)
    v_ref[...] = zv_ref[...].astype(BF16)


def qkv_prep(dims, z_main, z_kv, qn_g, kn_g, aw, kw, hd):
    tr, n, s = dims.tr, dims.n, dims.s
    half = hd // 2
    pairs = half // 2
    t = jnp.arange(s)
    inv = ROPE_THETA ** (-jnp.arange(pairs, dtype=F32) / pairs)
    ang = jnp.concatenate([(t // GRID_W).astype(F32)[:, None] * inv, (t % GRID_W).astype(F32)[:, None] * inv], -1)
    cos, sin = jnp.cos(ang), jnp.sin(ang)
    reps = LANES // hd
    cos_t = jnp.tile(jnp.concatenate([cos, cos], -1), (1, reps))
    sin_t = jnp.tile(jnp.concatenate([-sin, sin], -1), (1, reps))
    cos_t = jnp.concatenate([cos_t, jnp.ones((tr, LANES), F32)], axis=0)
    sin_t = jnp.concatenate([sin_t, jnp.zeros((tr, LANES), F32)], axis=0)
    per_seq = s // tr
    n_lat_tiles = dims.n_lat // tr
    tab = pl.BlockSpec((tr, LANES), lambda i: (jnp.where(i < n_lat_tiles, i % per_seq, per_seq), 0))
    assert kw % LANES == 0 and aw % kw == 0
    return pl.pallas_call(
        functools.partial(_qkv_prep_kernel, hd=hd),
        grid=(n // tr,),
        in_specs=[pl.BlockSpec((tr, aw), lambda i: (i, 0)),
                  pl.BlockSpec((tr, kw), lambda i: (i, 0)),
                  pl.BlockSpec((tr, kw), lambda i: (i, 1)),
                  tab, tab,
                  pl.BlockSpec((1, aw), lambda i: (0, 0)), pl.BlockSpec((1, kw), lambda i: (0, 0)),
                  pl.BlockSpec((LANES, LANES), lambda i: (0, 0))],
        out_specs=[pl.BlockSpec((tr, aw), lambda i: (i, 0)), pl.BlockSpec((tr, kw), lambda i: (i, 0)),
                   pl.BlockSpec((tr, kw), lambda i: (i, 0))],
        out_shape=[jax.ShapeDtypeStruct((n, aw), BF16), jax.ShapeDtypeStruct((n, kw), BF16),
                   jax.ShapeDtypeStruct((n, kw), BF16)],
        compiler_params=_params(("parallel",)),
        name="qkv_prep",
    )(z_main, z_kv, z_kv, cos_t, sin_t, jnp.tile(qn_g, aw // hd).reshape(1, aw),
      jnp.tile(kn_g, kw // hd).reshape(1, kw), _block_ones(hd))


def _flash_kernel(q_ref, k_ref, v_ref, o_ref, qs_scr, m_scr, l_scr, acc_scr, *, kvh, grp, hd, tq, nk):
    j = pl.program_id(2)

    @pl.when(j == 0)
    def _():
        for g in range(kvh):
            for u in range(grp):
                h = g * grp + u
                qs_scr[g, u * tq:(u + 1) * tq, :] = q_ref[:, h * hd:(h + 1) * hd]
        m_scr[...] = jnp.full_like(m_scr, -jnp.inf)
        l_scr[...] = jnp.zeros_like(l_scr)
        acc_scr[...] = jnp.zeros_like(acc_scr)

    for g in range(kvh):
        kg = k_ref[:, g * hd:(g + 1) * hd]
        vg = v_ref[:, g * hd:(g + 1) * hd]
        sc = lax.dot_general(qs_scr[g], kg, (((1,), (1,)), ((), ())), preferred_element_type=F32)
        m_prev = m_scr[g]
        m_new = jnp.maximum(m_prev, jnp.max(sc, axis=1, keepdims=True))
        alpha = jnp.exp(m_prev - m_new)
        pr = jnp.exp(sc - m_new)
        l_scr[g] = alpha * l_scr[g] + jnp.sum(pr, axis=1, keepdims=True)
        acc_scr[g] = alpha * acc_scr[g] + _dot(pr.astype(BF16), vg)
        m_scr[g] = m_new

    @pl.when(j == nk - 1)
    def _():
        for g in range(kvh):
            inv = 1.0 / l_scr[g]
            for u in range(grp):
                h = g * grp + u
                o_ref[:, h * hd:(h + 1) * hd] = (acc_scr[g, u * tq:(u + 1) * tq, :]
                                                 * inv[u * tq:(u + 1) * tq]).astype(BF16)


def flash_attention(q, k, v, hd, tq, tk, grid, q_map, kv_map, n_rows):
    aw, kw = q.shape[1], k.shape[1]
    kvh = kw // hd
    grp = aw // kw
    nk = grid[2]
    return pl.pallas_call(
        functools.partial(_flash_kernel, kvh=kvh, grp=grp, hd=hd, tq=tq, nk=nk),
        grid=grid,
        in_specs=[pl.BlockSpec((tq, aw), q_map), pl.BlockSpec((tk, kw), kv_map), pl.BlockSpec((tk, kw), kv_map)],
        out_specs=pl.BlockSpec((tq, aw), q_map),
        out_shape=jax.ShapeDtypeStruct((n_rows, aw), BF16),
        scratch_shapes=[pltpu.VMEM((kvh, grp * tq, hd), BF16), pltpu.VMEM((kvh, grp * tq, 1), F32),
                        pltpu.VMEM((kvh, grp * tq, 1), F32), pltpu.VMEM((kvh, grp * tq, hd), F32)],
        compiler_params=_params(("parallel", "parallel", "arbitrary")),
        name="flash_attention",
    )(q, k, v)


def attention(dims, q, k, v, hd, ctx_out):
    b, s, c, tr = dims.b, dims.s, dims.c, dims.tr
    lat_tiles = s // tr
    ctx0 = dims.n_lat // tr
    q_map = lambda bi, qi, kj: (bi * lat_tiles + qi, 0)
    kv_map = lambda bi, qi, kj: (jnp.where(kj == 0, ctx0 + bi, bi * lat_tiles + kj - 1), 0)
    att = flash_attention(q, k, v, hd, tr, tr, (b, lat_tiles, lat_tiles + 1), q_map, kv_map, dims.n_lat)
    if not ctx_out:
        return att
    cmap = lambda bi, qi, kj: (bi, 0)
    nl = dims.n_lat
    att_c = flash_attention(q[nl:], k[nl:], v[nl:], hd, tr, tr, (b, 1, 1), cmap, cmap, dims.n - nl)
    return jnp.concatenate([att, att_c], axis=0)


def _lru_prep_kernel(x_ref, xh_ref, cw_ref, cb_ref, wrh_ref, wrl_ref, brg_ref, wih_ref, wil_ref, big_ref,
                     lam_ref, a_ref, b_ref, x_scr, *, dims, reverse, taps):
    first, last = _seg_edges(dims, pl.program_id(0))
    tr = dims.tr
    if reverse:
        x_scr[0:tr, :] = x_ref[...]
        x_scr[tr:, :] = jnp.where(last, 0.0, xh_ref[...])
        base = 0
    else:
        x_scr[0:8, :] = jnp.where(first, 0.0, xh_ref[...])
        x_scr[8:, :] = x_ref[...]
        base = 8 - (taps - 1)
    xc = jnp.zeros((tr, x_scr.shape[1]), F32) + cb_ref[...]
    for j in range(taps):
        xc = xc + cw_ref[j:j + 1, :] * x_scr[pl.ds(base + j, tr), :]
    cwd = wrh_ref.shape[1]
    rg, ig = [], []
    for c in range(wrh_ref.shape[0]):
        xcc = xc[:, c * cwd:(c + 1) * cwd]
        rg.append(_dot3(xcc, wrh_ref[c], wrl_ref[c]))
        ig.append(_dot3(xcc, wih_ref[c], wil_ref[c]))
    cat = lambda xs: jnp.concatenate(xs, axis=1) if len(xs) > 1 else xs[0]
    rg = _sigmoid(cat(rg) + brg_ref[...])
    ig = _sigmoid(cat(ig) + big_ref[...])
    nl = -lam_ref[...]
    softplus = jnp.maximum(nl, 0.0) + jnp.log1p(jnp.exp(-jnp.abs(nl)))
    log_a = -LRU_C * rg * softplus
    a = jnp.exp(log_a)
    a_ref[...] = a
    b_ref[...] = jnp.sqrt(1.0 - a * a) * (ig * xc)


def _block_diag(wb, cwd):
    nb, bw, _ = wb.shape
    per = cwd // bw
    wb = wb.reshape(nb // per, per, bw, bw)
    eye = jnp.eye(per, dtype=wb.dtype)
    return jnp.einsum("cpij,pq->cpiqj", wb, eye).reshape(nb // per, cwd, cwd)


def lru_prep(dims, z_main, col, p, d_idx):
    reverse = d_idx == 1
    tr, n = dims.tr, dims.n
    taps, lw = p["conv_w"].shape[-2:]
    cwd = min(MXU_DIM, lw)
    wrh, wrl = _split(_block_diag(p["w_rg"][d_idx], cwd))
    wih, wil = _split(_block_diag(p["w_ig"][d_idx], cwd))
    per = tr // 8
    nb = n // 8
    if reverse:
        halo = pl.BlockSpec((8, lw), lambda i: (jnp.minimum((i + 1) * per, nb - 1), col))
    else:
        halo = pl.BlockSpec((8, lw), lambda i: (jnp.maximum(i * per - 1, 0), col))
    vec = pl.BlockSpec((1, lw), lambda i: (0, 0))
    mat = pl.BlockSpec((lw // cwd, cwd, cwd), lambda i: (0, 0, 0))
    return pl.pallas_call(
        functools.partial(_lru_prep_kernel, dims=dims, reverse=reverse, taps=taps),
        grid=(n // tr,),
        in_specs=[pl.BlockSpec((tr, lw), lambda i: (i, col)), halo,
                  pl.BlockSpec((taps, lw), lambda i: (0, 0)), vec, mat, mat, vec, mat, mat, vec, vec],
        out_specs=[pl.BlockSpec((tr, lw), lambda i: (i, 0))] * 2,
        out_shape=[jax.ShapeDtypeStruct((n, lw), F32)] * 2,
        scratch_shapes=[pltpu.VMEM((tr + 8, lw), F32)],
        compiler_params=_params(("parallel",)),
        name="lru_prep",
    )(z_main, z_main, p["conv_w"][d_idx], p["conv_b"][d_idx].reshape(1, lw), wrh, wrl,
      p["b_rg"][d_idx].reshape(1, lw), wih, wil, p["b_ig"][d_idx].reshape(1, lw),
      p["lam"][d_idx].reshape(1, lw))


def _lru_scan_kernel(a_ref, b_ref, h_ref, h_scr, *, tr):
    d = pl.program_id(0)

    @pl.when(pl.program_id(2) == 0)
    def _():
        h_scr[...] = jnp.zeros_like(h_scr)

    def step(s, h):
        t = jnp.where(d == 0, s, tr - 1 - s)
        h = a_ref[0, pl.ds(t, 1), :] * h + b_ref[0, pl.ds(t, 1), :]
        h_ref[0, pl.ds(t, 1), :] = h
        return h

    h_scr[...] = lax.fori_loop(0, tr, step, h_scr[...])


def lru_scan(dims, a, b):
    tr, s, bsz = dims.tr, dims.s, dims.b
    lw = a.shape[-1]
    lat_tiles = s // tr
    ctx0 = dims.n_lat // tr

    def tmap(d, bi, j):
        lat = jnp.where(d == 0, j - 1, lat_tiles - j)
        return (d, jnp.where(j == 0, ctx0 + bi, bi * lat_tiles + lat), 0)

    spec = pl.BlockSpec((1, tr, lw), tmap)
    return pl.pallas_call(
        functools.partial(_lru_scan_kernel, tr=tr),
        grid=(2, bsz, lat_tiles + 1),
        in_specs=[spec, spec],
        out_specs=spec,
        out_shape=jax.ShapeDtypeStruct(a.shape, F32),
        scratch_shapes=[pltpu.VMEM((1, lw), F32)],
        compiler_params=_params(("arbitrary", "arbitrary", "arbitrary")),
        name="lru_scan",
    )(a, b)


def _lru_gate_kernel(h_ref, g_ref, o_ref):
    g = g_ref[...]
    gelu = 0.5 * g * (1.0 + jnp.tanh(math.sqrt(2.0 / math.pi) * (g + 0.044715 * (g * g * g))))
    o_ref[...] = ((h_ref[0] + h_ref[1]) * gelu).astype(BF16)


def lru_gate(dims, h, z_main, col):
    tr, n = dims.tr, dims.n
    lw = h.shape[-1]
    return pl.pallas_call(
        _lru_gate_kernel,
        grid=(n // tr,),
        in_specs=[pl.BlockSpec((2, tr, lw), lambda i: (0, i, 0)), pl.BlockSpec((tr, lw), lambda i: (i, col))],
        out_specs=pl.BlockSpec((tr, lw), lambda i: (i, 0)),
        out_shape=jax.ShapeDtypeStruct((n, lw), BF16),
        compiler_params=_params(("parallel",)),
        name="lru_gate",
    )(h, z_main)


def odd_mixer(dims, x, g_norm, mod, p, hd, ctx_out, n_out_rows):
    n = dims.n
    lw = p["lam"].shape[-1]
    w_in = p["w_in"]
    aw = p["w_out"].shape[0] - lw
    kw = (w_in.shape[1] - aw - 2 * lw) // 2
    assert aw == lw
    main_cols = jnp.concatenate([w_in[:, :aw], w_in[:, aw + 2 * kw:]], axis=1)
    z_main = nm_matmul(dims, x, g_norm, mod["sh1"], mod["sc1"], main_cols.astype(BF16), n)
    z_kv = nm_matmul(dims, x, g_norm, mod["sh1"], mod["sc1"], w_in[:, aw:aw + 2 * kw].astype(BF16), n)
    q, k, v = qkv_prep(dims, z_main, z_kv, p["qn_g"], p["kn_g"], aw, kw, hd)
    att = attention(dims, q, k, v, hd, ctx_out)
    ab = [lru_prep(dims, z_main, 1, p, d_idx) for d_idx in range(2)]
    h = lru_scan(dims, jnp.stack([ab[0][0], ab[1][0]]), jnp.stack([ab[0][1], ab[1][1]]))
    hg = lru_gate(dims, h, z_main, 2)
    w_out = p["w_out"].astype(BF16)
    return matmul_residual(dims, att, hg, w_out[:aw], w_out[aw:], x, mod["g1"], n_out_rows)


def _moe_kernel(be_ref, na_ref, x_ref, wg_ref, wu_ref, wd_ref, gt_ref, o_ref):
    i = pl.program_id(0)

    @pl.when(i < na_ref[0])
    def _():
        x = x_ref[...]
        hg = _dot(x, wg_ref[0])
        hu = _dot(x, wu_ref[0])
        h = (hg * _sigmoid(hg) * hu).astype(BF16)
        o_ref[...] = _dot(h, wd_ref[0]) * gt_ref[...]

    @pl.when(i >= na_ref[0])
    def _():
        o_ref[...] = jnp.zeros_like(o_ref)


def moe_experts(xs, gate_rows, blk_expert, n_active, w_gate, w_up, w_down):
    rows, d = xs.shape
    n_blk = rows // MOE_ROWS
    de = w_gate.shape[-1]
    grid_spec = pltpu.PrefetchScalarGridSpec(
        num_scalar_prefetch=2,
        grid=(n_blk,),
        in_specs=[pl.BlockSpec((MOE_ROWS, d), lambda i, be, na: (i, 0)),
                  pl.BlockSpec((1, d, de), lambda i, be, na: (be[i], 0, 0)),
                  pl.BlockSpec((1, d, de), lambda i, be, na: (be[i], 0, 0)),
                  pl.BlockSpec((1, de, d), lambda i, be, na: (be[i], 0, 0)),
                  pl.BlockSpec((MOE_ROWS, 1), lambda i, be, na: (i, 0))],
        out_specs=pl.BlockSpec((MOE_ROWS, d), lambda i, be, na: (i, 0)),
    )
    return pl.pallas_call(
        _moe_kernel,
        grid_spec=grid_spec,
        out_shape=jax.ShapeDtypeStruct((rows, d), F32),
        compiler_params=_params(("arbitrary",)),
        name="moe_experts",
    )(blk_expert, n_active, xs, w_gate, w_up, w_down, gate_rows)


def hierarchical_moe(dims, x, g_norm, mod, p, n_rows):
    n_grp = p["w_grp"].shape[1]
    epg = p["w_router"].shape[-1]
    n_exp = n_grp * epg
    d = dims.d
    w_rt = jnp.concatenate([p["w_grp"], p["w_router"].transpose(1, 0, 2).reshape(d, n_exp)], axis=1)
    b_rt = jnp.concatenate([p["b_grp"], p["b_router"].reshape(n_exp)])
    padc = -w_rt.shape[1] % LANES
    f, logits = nm_router(dims, x, g_norm, mod["sh2"], mod["sc2"], jnp.pad(w_rt, ((0, 0), (0, padc))),
                          jnp.pad(b_rt, (0, padc)), n_rows)

    grp_p, grp_idx = lax.top_k(jax.nn.softmax(logits[:, :n_grp], axis=-1), 1)
    sel = jnp.take_along_axis(logits[:, n_grp:n_grp + n_exp].reshape(n_rows, n_grp, epg),
                              grp_idx[:, :, None], axis=1)[:, 0]
    top_p, top_e = lax.top_k(jax.nn.softmax(sel, axis=-1), TOP_K)
    gates = grp_p * top_p / jnp.sum(top_p, axis=-1, keepdims=True)
    flat_e = (grp_idx * epg + top_e).reshape(-1)
    n_assign = n_rows * TOP_K
    order = jnp.argsort(flat_e)
    e_sorted = flat_e[order]
    tok_sorted = (order // TOP_K).astype(jnp.int32)
    counts = jnp.zeros((n_exp,), jnp.int32).at[flat_e].add(1)
    padded = (counts + MOE_ROWS - 1) // MOE_ROWS * MOE_ROWS
    seg_start = jnp.cumsum(counts) - counts
    pad_end = jnp.cumsum(padded)
    slot = (pad_end - padded)[e_sorted] + jnp.arange(n_assign, dtype=jnp.int32) - seg_start[e_sorted]
    n_blk = -(-n_assign // MOE_ROWS) + n_exp
    blk_expert = jnp.minimum(jnp.searchsorted(pad_end // MOE_ROWS, jnp.arange(n_blk), side="right"),
                             n_exp - 1).astype(jnp.int32)
    n_active = (pad_end[-1] // MOE_ROWS).astype(jnp.int32).reshape(1)
    src = jnp.full((n_blk * MOE_ROWS,), n_rows, jnp.int32).at[slot].set(tok_sorted)
    gate_rows = jnp.zeros((n_blk * MOE_ROWS,), F32).at[slot].set(gates.reshape(-1)[order])
    slot_of = jnp.zeros((n_assign,), jnp.int32).at[order].set(slot).reshape(n_rows, TOP_K)

    xs = jnp.take(f, src, axis=0, mode="fill", fill_value=0)
    out = moe_experts(xs, gate_rows[:, None], blk_expert, n_active, p["w_gate"].astype(BF16),
                      p["w_up"].astype(BF16), p["w_down"].astype(BF16))
    y0 = jnp.take(out, slot_of[:, 0], axis=0)
    y1 = jnp.take(out, slot_of[:, 1], axis=0)
    return combine_residual(dims, x, y0, y1, mod["g2"], n_rows)


def kernel(x, c, ctx, c_ctx, norm_mix_g, norm_ffn_g, ada_w, ada_b, even_w_in, rwkv_mu, rwkv_w0, rwkv_w_up, rwkv_a0, rwkv_a_up, rwkv_k_k, rwkv_k_a, rwkv_r_k, rwkv_g_up, rwkv_gn_g, rwkv_gn_b, conv_w, conv_b, conv_ln_g, conv_ln_b, even_w_out, odd_w_in, att_qn_g, att_kn_g, lru_conv_w, lru_conv_b, lru_w_rg, lru_b_rg, lru_w_ig, lru_b_ig, lru_lambda, odd_w_out, moe_w_grp, moe_b_grp, moe_w_router, moe_b_router, moe_w_gate, moe_w_up, moe_w_down, final_g):
    bsz, s, d = x.shape
    n_ctx = ctx.shape[1]
    depth = norm_mix_g.shape[0]
    hd = att_qn_g.shape[-1]
    dims = _Dims(bsz, s, n_ctx, d)
    xa = jnp.concatenate([x.reshape(bsz * s, d), ctx.reshape(bsz * n_ctx, d)], axis=0)

    cond = jnp.concatenate([c, c_ctx[None, :]], axis=0)
    cond = cond * _sigmoid(cond)
    rows = -(bsz + 1) % 8
    cond = jnp.pad(cond, ((0, rows), (0, 0)))

    for layer in range(depth):
        ctx_out = layer < depth - 1
        n_rows = dims.n if ctx_out else dims.n_lat
        j = layer // 2
        mod_all = ada_matmul(cond, ada_w[layer], ada_b[layer])[:bsz + 1]
        names = ("sh1", "sc1", "g1", "sh2", "sc2", "g2")
        mod = {nm: mod_all[:, q * d:(q + 1) * d].reshape(bsz + 1, 1, d) for q, nm in enumerate(names)}
        if layer % 2 == 0:
            p = dict(w_in=even_w_in[j], mu=rwkv_mu[j], w0=rwkv_w0[j], w_up=rwkv_w_up[j], a0=rwkv_a0[j],
                     a_up=rwkv_a_up[j], k_k=rwkv_k_k[j], k_a=rwkv_k_a[j], r_k=rwkv_r_k[j].reshape(-1),
                     g_up=rwkv_g_up[j], gn_g=rwkv_gn_g[j], gn_b=rwkv_gn_b[j], conv_w=conv_w[j],
                     conv_b=conv_b[j], ln_g=conv_ln_g[j], ln_b=conv_ln_b[j], w_out=even_w_out[j])
            x_new = even_mixer(dims, xa, norm_mix_g[layer], mod, p, hd, n_rows)
        else:
            p = dict(w_in=odd_w_in[j], qn_g=att_qn_g[j], kn_g=att_kn_g[j], conv_w=lru_conv_w[j],
                     conv_b=lru_conv_b[j], w_rg=lru_w_rg[j], b_rg=lru_b_rg[j].reshape(2, -1),
                     w_ig=lru_w_ig[j], b_ig=lru_b_ig[j].reshape(2, -1), lam=lru_lambda[j], w_out=odd_w_out[j])
            x_new = odd_mixer(dims, xa, norm_mix_g[layer], mod, p, hd, ctx_out, n_rows)
        pm = dict(w_grp=moe_w_grp[layer], b_grp=moe_b_grp[layer], w_router=moe_w_router[layer],
                  b_router=moe_b_router[layer], w_gate=moe_w_gate[layer], w_up=moe_w_up[layer],
                  w_down=moe_w_down[layer])
        xa = hierarchical_moe(dims, x_new, norm_ffn_g[layer], mod, pm, n_rows)
    return final_norm(dims, xa, final_g, dims.n_lat).reshape(bsz, s, d)
```

```python
import functools
import math

import jax
import jax.numpy as jnp
from jax import lax
from jax.experimental import pallas as pl
from jax.experimental.pallas import tpu as pltpu

F32 = jnp.float32
BF16 = jnp.bfloat16

GRID_W = 64
ROPE_THETA = 10000.0
RWKV_GN_EPS = 64e-5
LRU_C = 8.0
TOP_K = 2
RMS_EPS = 1e-6
LN_EPS = 1e-5
LANES = 128
MXU_DIM = 256
VMEM_LIMIT = 48 * 2**20
MOE_ROWS = 256
SCAN_TB = 32


def _params(sem):
    return pltpu.CompilerParams(dimension_semantics=sem, vmem_limit_bytes=VMEM_LIMIT)


def _dot(a, b):
    return jnp.dot(a, b, preferred_element_type=F32)


def _split(w):
    hi = w.astype(BF16)
    return hi, (w - hi.astype(F32)).astype(BF16)


def _dot3(a, w_hi, w_lo):
    a_hi = a.astype(BF16)
    a_lo = (a - a_hi.astype(F32)).astype(BF16)
    return _dot(a_hi, w_hi) + _dot(a_lo, w_hi) + _dot(a_hi, w_lo)


def _sigmoid(x):
    return 1.0 / (1.0 + jnp.exp(-x))


def _pow2_tile(limit, *dims):
    t = limit
    while any(d % t for d in dims):
        t //= 2
    return t


class _Dims:
    def __init__(self, b, s, c, d):
        self.b, self.s, self.c, self.d = b, s, c, d
        self.n_lat = b * s
        self.n = b * s + b * c
        self.tm = _pow2_tile(1024, s, b * c)
        self.tr = _pow2_tile(256, s, c)
        assert self.tr == c, "sequence kernels assume one context tile per batch element"

    def mod_index(self, tile):
        n_lat_tiles = self.n_lat // tile

        def f(i):
            return jnp.where(i < n_lat_tiles, (i * tile) // self.s, self.b)
        return f

    def seq_index(self, i):
        per_seq = self.s // self.tr
        n_lat_tiles = self.n_lat // self.tr
        return jnp.where(i < n_lat_tiles, (i // per_seq) * (per_seq + 1) + 1 + i % per_seq,
                         (i - n_lat_tiles) * (per_seq + 1))


def _ada_kernel(a_ref, wh_ref, wl_ref, b_ref, o_ref):
    o_ref[...] = _dot3(a_ref[...], wh_ref[...], wl_ref[...]) + b_ref[...]


def ada_matmul(a, w, bias):
    r, d = a.shape
    m = w.shape[1]
    tn = _pow2_tile(1024, m)
    wh, wl = _split(w)
    return pl.pallas_call(
        _ada_kernel,
        grid=(m // tn,),
        in_specs=[pl.BlockSpec((r, d), lambda j: (0, 0)),
                  pl.BlockSpec((d, tn), lambda j: (0, j)),
                  pl.BlockSpec((d, tn), lambda j: (0, j)),
                  pl.BlockSpec((1, tn), lambda j: (0, j))],
        out_specs=pl.BlockSpec((r, tn), lambda j: (0, j)),
        out_shape=jax.ShapeDtypeStruct((r, m), F32),
        compiler_params=_params(("arbitrary",)),
        name="ada_matmul",
    )(a, wh, wl, bias.reshape(1, m))


def _norm_mod(x_ref, g_ref, sh_ref, sc_ref):
    xf = x_ref[...]
    ms = jnp.mean(xf * xf, axis=-1, keepdims=True)
    y = xf * lax.rsqrt(ms + RMS_EPS) * g_ref[...]
    return y * (1.0 + sc_ref[0]) + sh_ref[0]


def _nm_matmul_kernel(x_ref, g_ref, sh_ref, sc_ref, w_ref, z_ref, h_scr):
    @pl.when(pl.program_id(1) == 0)
    def _():
        h_scr[...] = _norm_mod(x_ref, g_ref, sh_ref, sc_ref).astype(BF16)

    z_ref[...] = _dot(h_scr[...], w_ref[...])


def nm_matmul(dims, x, g, shift, scale, w, n_rows):
    d = dims.d
    m = w.shape[1]
    tm = dims.tm
    tn = _pow2_tile(512, m)
    midx = dims.mod_index(tm)
    return pl.pallas_call(
        _nm_matmul_kernel,
        grid=(n_rows // tm, m // tn),
        in_specs=[pl.BlockSpec((tm, d), lambda i, j: (i, 0)),
                  pl.BlockSpec((1, d), lambda i, j: (0, 0)),
                  pl.BlockSpec((1, 1, d), lambda i, j: (midx(i), 0, 0)),
                  pl.BlockSpec((1, 1, d), lambda i, j: (midx(i), 0, 0)),
                  pl.BlockSpec((d, tn), lambda i, j: (0, j))],
        out_specs=pl.BlockSpec((tm, tn), lambda i, j: (i, j)),
        out_shape=jax.ShapeDtypeStruct((n_rows, m), F32),
        scratch_shapes=[pltpu.VMEM((tm, d), BF16)],
        compiler_params=_params(("parallel", "arbitrary")),
        name="nm_matmul",
    )(x, g.reshape(1, d), shift, scale, w)


def _nm_router_kernel(x_ref, g_ref, sh_ref, sc_ref, wh_ref, wl_ref, b_ref, f_ref, rt_ref, *, n_grp, epg):
    h = _norm_mod(x_ref, g_ref, sh_ref, sc_ref)
    f_ref[...] = h.astype(BF16)
    lg = _dot3(h, wh_ref[...], wl_ref[...]) + b_ref[...]
    lane = lax.broadcasted_iota(jnp.int32, (1, lg.shape[1]), 1).astype(F32)
    neg = -jnp.inf
    big = float(lg.shape[1])

    def first_max(vals):
        mx = jnp.max(vals, axis=-1, keepdims=True)
        return mx, jnp.min(jnp.where(vals == mx, lane, big), axis=-1, keepdims=True)

    gl = jnp.where(lane < n_grp, lg, neg)
    gmax, gidx = first_max(gl)
    grp_p = 1.0 / jnp.sum(jnp.exp(gl - gmax), axis=-1, keepdims=True)
    lo = n_grp + gidx * epg
    el = jnp.where(lane >= lo, jnp.where(lane < lo + epg, lg, neg), neg)
    m1, i1 = first_max(el)
    m2, i2 = first_max(jnp.where(lane == i1, neg, el))
    p2 = jnp.exp(m2 - m1)
    g1 = grp_p / (1.0 + p2)
    g2 = grp_p * p2 / (1.0 + p2)
    rt_ref[...] = jnp.where(lane == 0, g1, jnp.where(lane == 1, g2, jnp.where(
        lane == 2, i1 - n_grp, jnp.where(lane == 3, i2 - n_grp, 0.0))))


def nm_router(dims, x, g, shift, scale, w_rt, b_rt, n_rows, n_grp, epg):
    d = dims.d
    tm = dims.tm
    m = w_rt.shape[1]
    wh, wl = _split(w_rt)
    midx = dims.mod_index(tm)
    return pl.pallas_call(
        functools.partial(_nm_router_kernel, n_grp=n_grp, epg=epg),
        grid=(n_rows // tm,),
        in_specs=[pl.BlockSpec((tm, d), lambda i: (i, 0)),
                  pl.BlockSpec((1, d), lambda i: (0, 0)),
                  pl.BlockSpec((1, 1, d), lambda i: (midx(i), 0, 0)),
                  pl.BlockSpec((1, 1, d), lambda i: (midx(i), 0, 0)),
                  pl.BlockSpec((d, m), lambda i: (0, 0)),
                  pl.BlockSpec((d, m), lambda i: (0, 0)),
                  pl.BlockSpec((1, m), lambda i: (0, 0))],
        out_specs=[pl.BlockSpec((tm, d), lambda i: (i, 0)),
                   pl.BlockSpec((tm, m), lambda i: (i, 0))],
        out_shape=[jax.ShapeDtypeStruct((n_rows, d), BF16),
                   jax.ShapeDtypeStruct((n_rows, m), F32)],
        compiler_params=_params(("parallel",)),
        name="nm_router",
    )(x, g.reshape(1, d), shift, scale, wh, wl, b_rt.reshape(1, m))


def _mm_res_kernel(a1_ref, a2_ref, w1_ref, w2_ref, x_ref, gt_ref, o_ref):
    acc = _dot(a1_ref[...], w1_ref[...]) + _dot(a2_ref[...], w2_ref[...])
    o_ref[...] = x_ref[...] + gt_ref[0] * acc


def matmul_residual(dims, a1, a2, w1, w2, x, gate, n_rows):
    d = dims.d
    tm = dims.tm
    tn = _pow2_tile(512, d)
    k1, k2 = a1.shape[1], a2.shape[1]
    midx = dims.mod_index(tm)
    out = pl.pallas_call(
        _mm_res_kernel,
        grid=(n_rows // tm, d // tn),
        in_specs=[pl.BlockSpec((tm, k1), lambda i, j: (i, 0)),
                  pl.BlockSpec((tm, k2), lambda i, j: (i, 0)),
                  pl.BlockSpec((k1, tn), lambda i, j: (0, j)),
                  pl.BlockSpec((k2, tn), lambda i, j: (0, j)),
                  pl.BlockSpec((tm, tn), lambda i, j: (i, j)),
                  pl.BlockSpec((1, 1, tn), lambda i, j: (midx(i), 0, j))],
        out_specs=pl.BlockSpec((tm, tn), lambda i, j: (i, j)),
        out_shape=jax.ShapeDtypeStruct((n_rows, d), F32),
        compiler_params=_params(("parallel", "arbitrary")),
        name="matmul_residual",
    )(a1, a2, w1, w2, x, gate)
    return out


def _combine_kernel(x_ref, y0_ref, y1_ref, gt_ref, o_ref):
    o_ref[...] = x_ref[...] + gt_ref[0] * (y0_ref[...] + y1_ref[...])


def combine_residual(dims, x, y0, y1, gate, n_rows):
    d = dims.d
    tm = _pow2_tile(512, dims.tm)
    midx = dims.mod_index(tm)
    row = pl.BlockSpec((tm, d), lambda i: (i, 0))
    return pl.pallas_call(
        _combine_kernel,
        grid=(n_rows // tm,),
        in_specs=[row, row, row, pl.BlockSpec((1, 1, d), lambda i: (midx(i), 0, 0))],
        out_specs=row,
        out_shape=jax.ShapeDtypeStruct((n_rows, d), F32),
        compiler_params=_params(("parallel",)),
        name="combine_residual",
    )(x, y0, y1, gate)


def _final_norm_kernel(x_ref, g_ref, o_ref):
    xf = x_ref[...]
    ms = jnp.mean(xf * xf, axis=-1, keepdims=True)
    o_ref[...] = xf * lax.rsqrt(ms + RMS_EPS) * g_ref[...]


def final_norm(dims, x, g, n_rows):
    d = dims.d
    tm = _pow2_tile(512, dims.tm)
    return pl.pallas_call(
        _final_norm_kernel,
        grid=(n_rows // tm,),
        in_specs=[pl.BlockSpec((tm, d), lambda i: (i, 0)), pl.BlockSpec((1, d), lambda i: (0, 0))],
        out_specs=pl.BlockSpec((tm, d), lambda i: (i, 0)),
        out_shape=jax.ShapeDtypeStruct((n_rows, d), F32),
        compiler_params=_params(("parallel",)),
        name="final_norm",
    )(x, g.reshape(1, d))


def _seg_edges(dims, i):
    tr = dims.tr
    n_lat_tiles = dims.n_lat // tr
    per_seq = dims.s // tr
    is_lat = i < n_lat_tiles
    first = jnp.where(is_lat, i % per_seq == 0, True)
    last = jnp.where(is_lat, i % per_seq == per_seq - 1, True)
    return first, last


def _block_ones(hd):
    idx = jnp.arange(LANES) // hd
    return (idx[:, None] == idx[None, :]).astype(BF16)


def _seg_sum(x, q):
    outs = []
    for c in range(x.shape[1] // LANES):
        xc = x[:, c * LANES:(c + 1) * LANES]
        hi = xc.astype(BF16)
        lo = (xc - hi.astype(F32)).astype(BF16)
        outs.append(_dot(hi, q) + _dot(lo, q))
    return jnp.concatenate(outs, axis=1) if len(outs) > 1 else outs[0]


def _neighbour(x, halo_ref, reverse, at_edge):
    rows = x.shape[0]
    rid = lax.broadcasted_iota(jnp.int32, (rows, 1), 0)
    if reverse:
        edge_row = jnp.where(at_edge, 0.0, halo_ref[0:1, :])
        return jnp.where(rid == rows - 1, edge_row, pltpu.roll(x, rows - 1, axis=0))
    edge_row = jnp.where(at_edge, 0.0, halo_ref[halo_ref.shape[0] - 1:, :])
    return jnp.where(rid == 0, edge_row, pltpu.roll(x, 1, axis=0))


def _rwkv_prep_kernel(r_ref, k_ref, v_ref, lw_ref, la_ref, rh_ref, kh_ref, vh_ref, lwh_ref, lah_ref,
                      mur_ref, muk_ref, muv_ref, mulw_ref, mula_ref, w0_ref, wuh_ref, wul_ref,
                      a0_ref, auh_ref, aul_ref, kk_ref, ka_ref, rk_ref, q_ref,
                      ro_ref, wo_ref, ko_ref, vo_ref, kko_ref, kao_ref, bo_ref, *, dims, reverse):
    first, last = _seg_edges(dims, pl.program_id(0))
    at_edge = last if reverse else first

    def mix(x_ref, h_ref, mu_ref):
        x = x_ref[...]
        return x + (_neighbour(x, h_ref, reverse, at_edge) - x) * mu_ref[...]

    r = mix(r_ref, rh_ref, mur_ref)
    k = mix(k_ref, kh_ref, muk_ref)
    v = mix(v_ref, vh_ref, muv_ref)
    lw = mix(lw_ref, lwh_ref, mulw_ref)
    la = mix(la_ref, lah_ref, mula_ref)
    q = q_ref[...]
    u = w0_ref[...] + _dot3(jnp.tanh(lw), wuh_ref[...], wul_ref[...])
    decay = jnp.exp(-math.exp(-0.5) * _sigmoid(u))
    a = _sigmoid(a0_ref[...] + _dot3(la, auh_ref[...], aul_ref[...]))
    kk = k * kk_ref[...]
    kk = kk * lax.rsqrt(jnp.maximum(_seg_sum(kk * kk, q), 1e-24))
    kf = k * (1.0 + (a - 1.0) * ka_ref[...])
    ro_ref[...] = r
    wo_ref[...] = decay
    ko_ref[...] = kf
    vo_ref[...] = v
    kko_ref[...] = kk
    kao_ref[...] = kk * a
    bo_ref[...] = _seg_sum(r * kf * rk_ref[...], q) * v


def rwkv_prep(dims, z_rkv, z_lora, p, d_idx, hd):
    reverse = d_idx == 1
    tr, n = dims.tr, dims.n
    w = p["w0"].shape[-1]
    lw_w = p["w_up"].shape[-2]
    la_w = p["a_up"].shape[-2]
    assert 2 * lw_w == LANES and 2 * la_w == LANES
    mu = p["mu"][d_idx]

    def half(vec, width):
        z = jnp.zeros_like(vec)
        return jnp.concatenate([vec, z] if d_idx == 0 else [z, vec], axis=0)

    mu_lw = half(mu[3 * w:3 * w + lw_w], lw_w).reshape(1, LANES)
    mu_la = half(mu[3 * w + lw_w:], la_w).reshape(1, LANES)
    wuh, wul = _split(half(p["w_up"][d_idx], lw_w))
    auh, aul = _split(half(p["a_up"][d_idx], la_w))
    row = lambda c: pl.BlockSpec((tr, w), lambda i: (i, c))
    chunk = lambda c: pl.BlockSpec((tr, LANES), lambda i: (i, c))

    def halo_c(c):
        per = tr // 8
        nb = n // 8
        if reverse:
            return pl.BlockSpec((8, LANES), lambda i: (jnp.minimum((i + 1) * per, nb - 1), c))
        return pl.BlockSpec((8, LANES), lambda i: (jnp.maximum(i * per - 1, 0), c))

    def halo_w(c):
        per = tr // 8
        nb = n // 8
        if reverse:
            return pl.BlockSpec((8, w), lambda i: (jnp.minimum((i + 1) * per, nb - 1), c))
        return pl.BlockSpec((8, w), lambda i: (jnp.maximum(i * per - 1, 0), c))

    vec = lambda width: pl.BlockSpec((1, width), lambda i: (0, 0))
    mat = lambda: pl.BlockSpec((LANES, w), lambda i: (0, 0))
    outs = pl.pallas_call(
        functools.partial(_rwkv_prep_kernel, dims=dims, reverse=reverse),
        grid=(n // tr,),
        in_specs=[row(0), row(1), row(2), chunk(0), chunk(1),
                  halo_w(0), halo_w(1), halo_w(2), halo_c(0), halo_c(1),
                  vec(w), vec(w), vec(w), vec(LANES), vec(LANES), vec(w), mat(), mat(),
                  vec(w), mat(), mat(), vec(w), vec(w), vec(w),
                  pl.BlockSpec((LANES, LANES), lambda i: (0, 0))],
        out_specs=[pl.BlockSpec((tr, w), lambda i: (dims.seq_index(i), 0))] * 6
        + [pl.BlockSpec((tr, w), lambda i: (i, 0))],
        out_shape=[jax.ShapeDtypeStruct((n, w), F32)] * 7,
        compiler_params=_params(("parallel",)),
        name="rwkv_prep",
    )(z_rkv, z_rkv, z_rkv, z_lora, z_lora, z_rkv, z_rkv, z_rkv, z_lora, z_lora,
      mu[:w].reshape(1, w), mu[w:2 * w].reshape(1, w), mu[2 * w:3 * w].reshape(1, w), mu_lw, mu_la,
      p["w0"][d_idx].reshape(1, w), wuh, wul, p["a0"][d_idx].reshape(1, w), auh, aul,
      p["k_k"].reshape(1, w), p["k_a"].reshape(1, w), p["r_k"].reshape(1, w), _block_ones(hd))
    return outs


def _rwkv_scan_kernel(r_ref, w_ref, k_ref, v_ref, kk_ref, ka_ref, y_ref, s_ref, wr_ref, *, tb, hd, reverse):
    @pl.when(pl.program_id(0) == 0)
    def _():
        s_ref[...] = jnp.zeros_like(s_ref)

    def step(s, carry):
        t = tb - 1 - s if reverse else s
        r_t = r_ref[t]
        v_t = v_ref[t]
        wr_ref[...] = w_ref[t] * r_t
        c1 = jnp.sum(ka_ref[t] * r_t, axis=0, keepdims=True)
        c2 = jnp.sum(k_ref[t] * r_t, axis=0, keepdims=True)
        acc_sa = jnp.zeros_like(v_t)
        acc_u = jnp.zeros_like(v_t)
        for kx in range(hd):
            sk = s_ref[kx]
            acc_sa = acc_sa + sk * kk_ref[t, pl.ds(kx, 1), :]
            acc_u = acc_u + sk * wr_ref[pl.ds(kx, 1), :]
        sa = -acc_sa
        y_ref[t] = acc_u + sa * c1 + v_t * c2
        for kx in range(hd):
            s_ref[kx] = (s_ref[kx] * w_ref[t, pl.ds(kx, 1), :] + sa * ka_ref[t, pl.ds(kx, 1), :]
                         + v_t * k_ref[t, pl.ds(kx, 1), :])
        return carry

    lax.fori_loop(0, tb, step, 0)


def rwkv_scan(xs, n_ctx, reverse):
    t_all, hd, lanes = xs[0].shape
    tb = _pow2_tile(SCAN_TB, n_ctx, t_all - n_ctx)
    n_blocks = t_all // tb
    n_ctx_blocks = n_ctx // tb

    def tmap(i):
        if not reverse:
            return (i, 0, 0)
        return (jnp.where(i < n_ctx_blocks, n_ctx_blocks - 1 - i, n_blocks - 1 - (i - n_ctx_blocks)), 0, 0)

    spec = pl.BlockSpec((tb, hd, lanes), tmap)
    return pl.pallas_call(
        functools.partial(_rwkv_scan_kernel, tb=tb, hd=hd, reverse=reverse),
        grid=(n_blocks,),
        in_specs=[spec] * 6,
        out_specs=spec,
        out_shape=jax.ShapeDtypeStruct(xs[0].shape, F32),
        scratch_shapes=[pltpu.VMEM((hd, hd, lanes), F32), pltpu.VMEM((hd, lanes), F32)],
        compiler_params=_params(("arbitrary",)),
        name="rwkv_scan",
    )(*xs)


def _rwkv_gn_kernel(y0_ref, y1_ref, g_ref, b_ref, o_ref):
    y = y0_ref[...] + y1_ref[...]
    mean = jnp.mean(y, axis=1, keepdims=True)
    yc = y - mean
    var = jnp.mean(yc * yc, axis=1, keepdims=True)
    o_ref[...] = yc * lax.rsqrt(var + RWKV_GN_EPS) * g_ref[...] + b_ref[...]


def rwkv_group_norm(y0, y1, g_tab, b_tab):
    t_all, hd, lanes = y0.shape
    tb = _pow2_tile(64, t_all)
    blk = pl.BlockSpec((tb, hd, lanes), lambda i: (i, 0, 0))
    tab = pl.BlockSpec((1, hd, lanes), lambda i: (0, 0, 0))
    return pl.pallas_call(
        _rwkv_gn_kernel,
        grid=(t_all // tb,),
        in_specs=[blk, blk, tab, tab],
        out_specs=blk,
        out_shape=jax.ShapeDtypeStruct((t_all, hd, lanes), F32),
        compiler_params=_params(("parallel",)),
        name="rwkv_group_norm",
    )(y0, y1, g_tab, b_tab)


def _even_merge_kernel(yn_ref, b0_ref, b1_ref, zg_ref, val_ref, gate_ref, valp_ref, gatep_ref, valn_ref,
                       gaten_ref, guh_ref, gul_ref, cw_ref, cb_ref, lng_ref, lnb_ref,
                       a_ref, bo_ref, u_scr, *, dims, taps, halo):
    first, last = _seg_edges(dims, pl.program_id(0))
    tr = dims.tr
    g = _dot3(_sigmoid(zg_ref[...]), guh_ref[...], gul_ref[...])
    a_ref[...] = ((yn_ref[...] + b0_ref[...] + b1_ref[...]) * g).astype(BF16)

    glu = lambda v_ref, g_ref: v_ref[...] * _sigmoid(g_ref[...])
    u_scr[0:halo, :] = jnp.where(first, 0.0, glu(valp_ref, gatep_ref))
    u_scr[halo:halo + tr, :] = glu(val_ref, gate_ref)
    u_scr[halo + tr:, :] = jnp.where(last, 0.0, glu(valn_ref, gaten_ref))
    pad = taps // 2
    acc = jnp.zeros((tr, u_scr.shape[1]), F32) + cb_ref[...]
    for j in range(taps):
        acc = acc + cw_ref[j:j + 1, :] * u_scr[pl.ds(halo - pad + j, tr), :]
    mu = jnp.mean(acc, axis=-1, keepdims=True)
    xc = acc - mu
    var = jnp.mean(xc * xc, axis=-1, keepdims=True)
    y = xc * lax.rsqrt(var + LN_EPS) * lng_ref[...] + lnb_ref[...]
    bo_ref[...] = (y * _sigmoid(y)).astype(BF16)


def even_merge(dims, yn, bonus0, bonus1, z_lora, z_cv, p):
    tr, n = dims.tr, dims.n
    w = yn.shape[1]
    taps, cw = p["conv_w"].shape
    lg = p["g_up"].shape[0]
    assert lg == LANES
    halo = 16
    assert taps // 2 <= halo and tr % halo == 0
    guh, gul = _split(p["g_up"])
    per = tr // halo
    nb = n // halo
    row_w = pl.BlockSpec((tr, w), lambda i: (i, 0))
    col = lambda c: pl.BlockSpec((tr, cw), lambda i: (i, c))
    prev = lambda c: pl.BlockSpec((halo, cw), lambda i: (jnp.maximum(i * per - 1, 0), c))
    nxt = lambda c: pl.BlockSpec((halo, cw), lambda i: (jnp.minimum((i + 1) * per, nb - 1), c))
    vec = lambda width: pl.BlockSpec((1, width), lambda i: (0, 0))
    return pl.pallas_call(
        functools.partial(_even_merge_kernel, dims=dims, taps=taps, halo=halo),
        grid=(n // tr,),
        in_specs=[pl.BlockSpec((tr, w), lambda i: (dims.seq_index(i), 0)), row_w, row_w,
                  pl.BlockSpec((tr, LANES), lambda i: (i, 2)),
                  col(0), col(1), prev(0), prev(1), nxt(0), nxt(1),
                  pl.BlockSpec((lg, w), lambda i: (0, 0)), pl.BlockSpec((lg, w), lambda i: (0, 0)),
                  pl.BlockSpec((taps, cw), lambda i: (0, 0)), vec(cw), vec(cw), vec(cw)],
        out_specs=[pl.BlockSpec((tr, w), lambda i: (i, 0)), pl.BlockSpec((tr, cw), lambda i: (i, 0))],
        out_shape=[jax.ShapeDtypeStruct((n, w), BF16), jax.ShapeDtypeStruct((n, cw), BF16)],
        scratch_shapes=[pltpu.VMEM((tr + 2 * halo, cw), F32)],
        compiler_params=_params(("parallel",)),
        name="even_merge",
    )(yn, bonus0, bonus1, z_lora, z_cv, z_cv, z_cv, z_cv, z_cv, z_cv, guh, gul,
      p["conv_w"], p["conv_b"].reshape(1, cw), p["ln_g"].reshape(1, cw), p["ln_b"].reshape(1, cw))


def even_mixer(dims, x, g_norm, mod, p, hd, n_out_rows):
    b, s, c, n = dims.b, dims.s, dims.c, dims.n
    w = p["w0"].shape[-1]
    cw = p["conv_w"].shape[1]
    lw_w, la_w, lg = p["w_up"].shape[-2], p["a_up"].shape[-2], p["g_up"].shape[0]
    w_in = p["w_in"]
    o = 3 * w
    lora_cols = w_in[:, o:o + 2 * lw_w + 2 * la_w + lg]
    lora_pad = jnp.pad(lora_cols, ((0, 0), (0, -lora_cols.shape[1] % (4 * LANES))))
    rwkv_in = o + 2 * lw_w + 2 * la_w + lg
    z_rkv = nm_matmul(dims, x, g_norm, mod["sh1"], mod["sc1"], w_in[:, :o].astype(BF16), n)
    z_lora = nm_matmul(dims, x, g_norm, mod["sh1"], mod["sc1"], lora_pad.astype(BF16), n)
    z_cv = nm_matmul(dims, x, g_norm, mod["sh1"], mod["sc1"], w_in[:, rwkv_in:].astype(BF16), n)

    heads = w // hd
    t_all = c + s

    def to_scan(a):
        return a.reshape(b, t_all, heads, hd).transpose(1, 3, 0, 2).reshape(t_all, hd, b * heads)

    per_dir = [rwkv_prep(dims, z_rkv, z_lora, p, d_idx, hd) for d_idx in range(2)]
    ys = [rwkv_scan([to_scan(per_dir[d_idx][q]) for q in range(6)], c, d_idx == 1) for d_idx in range(2)]

    def head_tab(vec):
        return jnp.tile(vec.reshape(heads, hd).T[:, None, :], (1, b, 1)).reshape(1, hd, b * heads)

    yn = rwkv_group_norm(ys[0], ys[1], head_tab(p["gn_g"]), head_tab(p["gn_b"]))
    yn = yn.reshape(t_all, hd, b, heads).transpose(2, 0, 3, 1).reshape(b * t_all, w)
    a_out, b_out = even_merge(dims, yn, per_dir[0][6], per_dir[1][6], z_lora, z_cv, p)
    w_out = p["w_out"].astype(BF16)
    return matmul_residual(dims, a_out, b_out, w_out[:w], w_out[w:], x, mod["g1"], n_out_rows)


def _qkv_prep_kernel(zq_ref, zk_ref, zv_ref, cos_ref, sin_ref, gq_ref, gk_ref, qm_ref,
                     q_ref, k_ref, v_ref, *, hd):
    qm = qm_ref[...]
    cos = cos_ref[...]
    sin = sin_ref[...]
    lane = lax.broadcasted_iota(jnp.int32, (1, LANES), 1)
    low = (lane % hd) < hd // 2

    def norm_rope(z, gain, scale):
        ms = _seg_sum(z * z, qm) * (1.0 / hd)
        y = z * lax.rsqrt(ms + RMS_EPS) * gain
        outs = []
        for c in range(y.shape[1] // LANES):
            yc = y[:, c * LANES:(c + 1) * LANES]
            partner = jnp.where(low, pltpu.roll(yc, LANES - hd // 2, axis=1), pltpu.roll(yc, hd // 2, axis=1))
            outs.append((yc * cos + partner * sin) * scale)
        return jnp.concatenate(outs, axis=1) if len(outs) > 1 else outs[0]

    q_ref[...] = norm_rope(zq_ref[...], gq_ref[...], hd ** -0.5).astype(BF16)
    k_ref[...] = norm_rope(zk_ref[...], gk_ref[...], 1.0).astype(BF16)
    v_ref[...] = zv_ref[...].astype(BF16)


def qkv_prep(dims, z_main, z_kv, qn_g, kn_g, aw, kw, hd):
    tr, n, s = dims.tr, dims.n, dims.s
    half = hd // 2
    pairs = half // 2
    t = jnp.arange(s)
    inv = ROPE_THETA ** (-jnp.arange(pairs, dtype=F32) / pairs)
    ang = jnp.concatenate([(t // GRID_W).astype(F32)[:, None] * inv, (t % GRID_W).astype(F32)[:, None] * inv], -1)
    cos, sin = jnp.cos(ang), jnp.sin(ang)
    reps = LANES // hd
    cos_t = jnp.tile(jnp.concatenate([cos, cos], -1), (1, reps))
    sin_t = jnp.tile(jnp.concatenate([-sin, sin], -1), (1, reps))
    cos_t = jnp.concatenate([cos_t, jnp.ones((tr, LANES), F32)], axis=0)
    sin_t = jnp.concatenate([sin_t, jnp.zeros((tr, LANES), F32)], axis=0)
    per_seq = s // tr
    n_lat_tiles = dims.n_lat // tr
    tab = pl.BlockSpec((tr, LANES), lambda i: (jnp.where(i < n_lat_tiles, i % per_seq, per_seq), 0))
    assert kw % LANES == 0 and aw % kw == 0
    return pl.pallas_call(
        functools.partial(_qkv_prep_kernel, hd=hd),
        grid=(n // tr,),
        in_specs=[pl.BlockSpec((tr, aw), lambda i: (i, 0)),
                  pl.BlockSpec((tr, kw), lambda i: (i, 0)),
                  pl.BlockSpec((tr, kw), lambda i: (i, 1)),
                  tab, tab,
                  pl.BlockSpec((1, aw), lambda i: (0, 0)), pl.BlockSpec((1, kw), lambda i: (0, 0)),
                  pl.BlockSpec((LANES, LANES), lambda i: (0, 0))],
        out_specs=[pl.BlockSpec((tr, aw), lambda i: (i, 0)), pl.BlockSpec((tr, kw), lambda i: (i, 0)),
                   pl.BlockSpec((tr, kw), lambda i: (i, 0))],
        out_shape=[jax.ShapeDtypeStruct((n, aw), BF16), jax.ShapeDtypeStruct((n, kw), BF16),
                   jax.ShapeDtypeStruct((n, kw), BF16)],
        compiler_params=_params(("parallel",)),
        name="qkv_prep",
    )(z_main, z_kv, z_kv, cos_t, sin_t, jnp.tile(qn_g, aw // hd).reshape(1, aw),
      jnp.tile(kn_g, kw // hd).reshape(1, kw), _block_ones(hd))


def _flash_kernel(qt_ref, k_ref, vt_ref, o_ref, qz_scr, m_scr, l_scr, acc_scr, *, kvh, grp, hd, tq, nk):
    j = pl.program_id(2)

    @pl.when(j == 0)
    def _():
        qz_scr[...] = jnp.zeros_like(qz_scr)
        for g in range(kvh):
            for u in range(grp):
                h = g * grp + u
                qz_scr[g, g * hd:(g + 1) * hd, u * tq:(u + 1) * tq] = qt_ref[h * hd:(h + 1) * hd, :]
        m_scr[...] = jnp.full_like(m_scr, -jnp.inf)
        l_scr[...] = jnp.zeros_like(l_scr)
        acc_scr[...] = jnp.zeros_like(acc_scr)

    k = k_ref[...]
    for g in range(kvh):
        st = _dot(k, qz_scr[g])
        m_prev = m_scr[g]
        m_new = jnp.maximum(m_prev, jnp.max(st, axis=0, keepdims=True))
        alpha = jnp.exp(m_prev - m_new)
        pt = jnp.exp(st - m_new)
        l_scr[g] = alpha * l_scr[g] + jnp.sum(pt, axis=0, keepdims=True)
        acc_scr[g] = alpha * acc_scr[g] + _dot(vt_ref[g * hd:(g + 1) * hd, :], pt.astype(BF16))
        m_scr[g] = m_new

    @pl.when(j == nk - 1)
    def _():
        for g in range(kvh):
            out = acc_scr[g] * (1.0 / l_scr[g])
            for u in range(grp):
                h = g * grp + u
                o_ref[h * hd:(h + 1) * hd, :] = out[:, u * tq:(u + 1) * tq].astype(BF16)


def flash_attention(qt, k, vt, hd, tq, tk, grid, q_map, k_map, vt_map):
    aw, kw = qt.shape[0], k.shape[1]
    kvh = kw // hd
    grp = aw // kw
    nk = grid[2]
    return pl.pallas_call(
        functools.partial(_flash_kernel, kvh=kvh, grp=grp, hd=hd, tq=tq, nk=nk),
        grid=grid,
        in_specs=[pl.BlockSpec((aw, tq), q_map), pl.BlockSpec((tk, kw), k_map), pl.BlockSpec((kw, tk), vt_map)],
        out_specs=pl.BlockSpec((aw, tq), q_map),
        out_shape=jax.ShapeDtypeStruct(qt.shape, BF16),
        scratch_shapes=[pltpu.VMEM((kvh, kw, grp * tq), BF16), pltpu.VMEM((kvh, 1, grp * tq), F32),
                        pltpu.VMEM((kvh, 1, grp * tq), F32), pltpu.VMEM((kvh, hd, grp * tq), F32)],
        compiler_params=_params(("parallel", "parallel", "arbitrary")),
        name="flash_attention",
    )(qt, k, vt)


def attention(dims, q, k, v, hd, ctx_out):
    b, s, c, tr = dims.b, dims.s, dims.c, dims.tr
    lat_tiles = s // tr
    ctx0 = dims.n_lat // tr
    nl = dims.n_lat
    qt, vt = q.T, v.T
    kv_tile = lambda bi, kj: jnp.where(kj == 0, ctx0 + bi, bi * lat_tiles + kj - 1)
    att = flash_attention(qt[:, :nl], k, vt, hd, tr, tr, (b, lat_tiles, lat_tiles + 1),
                          lambda bi, qi, kj: (0, bi * lat_tiles + qi),
                          lambda bi, qi, kj: (kv_tile(bi, kj), 0),
                          lambda bi, qi, kj: (0, kv_tile(bi, kj)))
    if not ctx_out:
        return att.T
    att_c = flash_attention(qt[:, nl:], k[nl:], vt[:, nl:], hd, tr, tr, (b, 1, 1),
                            lambda bi, qi, kj: (0, bi), lambda bi, qi, kj: (bi, 0), lambda bi, qi, kj: (0, bi))
    return jnp.concatenate([att, att_c], axis=1).T


def _lru_prep_kernel(x_ref, xh_ref, cw_ref, cb_ref, wrh_ref, wrl_ref, brg_ref, wih_ref, wil_ref, big_ref,
                     lam_ref, a_ref, b_ref, x_scr, *, dims, reverse, taps):
    first, last = _seg_edges(dims, pl.program_id(0))
    tr = dims.tr
    if reverse:
        x_scr[0:tr, :] = x_ref[...]
        x_scr[tr:, :] = jnp.where(last, 0.0, xh_ref[...])
        base = 0
    else:
        x_scr[0:8, :] = jnp.where(first, 0.0, xh_ref[...])
        x_scr[8:, :] = x_ref[...]
        base = 8 - (taps - 1)
    xc = jnp.zeros((tr, x_scr.shape[1]), F32) + cb_ref[...]
    for j in range(taps):
        xc = xc + cw_ref[j:j + 1, :] * x_scr[pl.ds(base + j, tr), :]
    cwd = wrh_ref.shape[1]
    rg, ig = [], []
    for c in range(wrh_ref.shape[0]):
        xcc = xc[:, c * cwd:(c + 1) * cwd]
        rg.append(_dot3(xcc, wrh_ref[c], wrl_ref[c]))
        ig.append(_dot3(xcc, wih_ref[c], wil_ref[c]))
    cat = lambda xs: jnp.concatenate(xs, axis=1) if len(xs) > 1 else xs[0]
    rg = _sigmoid(cat(rg) + brg_ref[...])
    ig = _sigmoid(cat(ig) + big_ref[...])
    nl = -lam_ref[...]
    softplus = jnp.maximum(nl, 0.0) + jnp.log1p(jnp.exp(-jnp.abs(nl)))
    log_a = -LRU_C * rg * softplus
    a = jnp.exp(log_a)
    a_ref[...] = a
    b_ref[...] = jnp.sqrt(1.0 - a * a) * (ig * xc)


def _block_diag(wb, cwd):
    nb, bw, _ = wb.shape
    per = cwd // bw
    wb = wb.reshape(nb // per, per, bw, bw)
    eye = jnp.eye(per, dtype=wb.dtype)
    return jnp.einsum("cpij,pq->cpiqj", wb, eye).reshape(nb // per, cwd, cwd)


def lru_prep(dims, z_main, col, p, d_idx):
    reverse = d_idx == 1
    tr, n = dims.tr, dims.n
    taps, lw = p["conv_w"].shape[-2:]
    cwd = min(MXU_DIM, lw)
    wrh, wrl = _split(_block_diag(p["w_rg"][d_idx], cwd))
    wih, wil = _split(_block_diag(p["w_ig"][d_idx], cwd))
    per = tr // 8
    nb = n // 8
    if reverse:
        halo = pl.BlockSpec((8, lw), lambda i: (jnp.minimum((i + 1) * per, nb - 1), col))
    else:
        halo = pl.BlockSpec((8, lw), lambda i: (jnp.maximum(i * per - 1, 0), col))
    vec = pl.BlockSpec((1, lw), lambda i: (0, 0))
    mat = pl.BlockSpec((lw // cwd, cwd, cwd), lambda i: (0, 0, 0))
    return pl.pallas_call(
        functools.partial(_lru_prep_kernel, dims=dims, reverse=reverse, taps=taps),
        grid=(n // tr,),
        in_specs=[pl.BlockSpec((tr, lw), lambda i: (i, col)), halo,
                  pl.BlockSpec((taps, lw), lambda i: (0, 0)), vec, mat, mat, vec, mat, mat, vec, vec],
        out_specs=[pl.BlockSpec((tr, lw), lambda i: (i, 0))] * 2,
        out_shape=[jax.ShapeDtypeStruct((n, lw), F32)] * 2,
        scratch_shapes=[pltpu.VMEM((tr + 8, lw), F32)],
        compiler_params=_params(("parallel",)),
        name="lru_prep",
    )(z_main, z_main, p["conv_w"][d_idx], p["conv_b"][d_idx].reshape(1, lw), wrh, wrl,
      p["b_rg"][d_idx].reshape(1, lw), wih, wil, p["b_ig"][d_idx].reshape(1, lw),
      p["lam"][d_idx].reshape(1, lw))


def _lru_scan_kernel(a_ref, b_ref, h_ref, h_scr, *, tr):
    d = pl.program_id(0)

    @pl.when(pl.program_id(2) == 0)
    def _():
        h_scr[...] = jnp.zeros_like(h_scr)

    def step(s, h):
        t = jnp.where(d == 0, s, tr - 1 - s)
        h = a_ref[0, pl.ds(t, 1), :] * h + b_ref[0, pl.ds(t, 1), :]
        h_ref[0, pl.ds(t, 1), :] = h
        return h

    h_scr[...] = lax.fori_loop(0, tr, step, h_scr[...])


def lru_scan(dims, a, b):
    tr, s, bsz = dims.tr, dims.s, dims.b
    lw = a.shape[-1]
    lat_tiles = s // tr
    ctx0 = dims.n_lat // tr

    def tmap(d, bi, j):
        lat = jnp.where(d == 0, j - 1, lat_tiles - j)
        return (d, jnp.where(j == 0, ctx0 + bi, bi * lat_tiles + lat), 0)

    spec = pl.BlockSpec((1, tr, lw), tmap)
    return pl.pallas_call(
        functools.partial(_lru_scan_kernel, tr=tr),
        grid=(2, bsz, lat_tiles + 1),
        in_specs=[spec, spec],
        out_specs=spec,
        out_shape=jax.ShapeDtypeStruct(a.shape, F32),
        scratch_shapes=[pltpu.VMEM((1, lw), F32)],
        compiler_params=_params(("arbitrary", "arbitrary", "arbitrary")),
        name="lru_scan",
    )(a, b)


def _lru_gate_kernel(h_ref, g_ref, o_ref):
    g = g_ref[...]
    gelu = 0.5 * g * (1.0 + jnp.tanh(math.sqrt(2.0 / math.pi) * (g + 0.044715 * (g * g * g))))
    o_ref[...] = ((h_ref[0] + h_ref[1]) * gelu).astype(BF16)


def lru_gate(dims, h, z_main, col):
    tr, n = dims.tr, dims.n
    lw = h.shape[-1]
    return pl.pallas_call(
        _lru_gate_kernel,
        grid=(n // tr,),
        in_specs=[pl.BlockSpec((2, tr, lw), lambda i: (0, i, 0)), pl.BlockSpec((tr, lw), lambda i: (i, col))],
        out_specs=pl.BlockSpec((tr, lw), lambda i: (i, 0)),
        out_shape=jax.ShapeDtypeStruct((n, lw), BF16),
        compiler_params=_params(("parallel",)),
        name="lru_gate",
    )(h, z_main)


def odd_mixer(dims, x, g_norm, mod, p, hd, ctx_out, n_out_rows):
    n = dims.n
    lw = p["lam"].shape[-1]
    w_in = p["w_in"]
    aw = p["w_out"].shape[0] - lw
    kw = (w_in.shape[1] - aw - 2 * lw) // 2
    assert aw == lw
    main_cols = jnp.concatenate([w_in[:, :aw], w_in[:, aw + 2 * kw:]], axis=1)
    z_main = nm_matmul(dims, x, g_norm, mod["sh1"], mod["sc1"], main_cols.astype(BF16), n)
    z_kv = nm_matmul(dims, x, g_norm, mod["sh1"], mod["sc1"], w_in[:, aw:aw + 2 * kw].astype(BF16), n)
    q, k, v = qkv_prep(dims, z_main, z_kv, p["qn_g"], p["kn_g"], aw, kw, hd)
    att = attention(dims, q, k, v, hd, ctx_out)
    ab = [lru_prep(dims, z_main, 1, p, d_idx) for d_idx in range(2)]
    h = lru_scan(dims, jnp.stack([ab[0][0], ab[1][0]]), jnp.stack([ab[0][1], ab[1][1]]))
    hg = lru_gate(dims, h, z_main, 2)
    w_out = p["w_out"].astype(BF16)
    return matmul_residual(dims, att, hg, w_out[:aw], w_out[aw:], x, mod["g1"], n_out_rows)


def _moe_kernel(be_ref, na_ref, x_ref, wg_ref, wu_ref, wd_ref, gt_ref, o_ref):
    i = pl.program_id(0)

    @pl.when(i < na_ref[0])
    def _():
        x = x_ref[...]
        hg = _dot(x, wg_ref[0])
        hu = _dot(x, wu_ref[0])
        h = (hg * _sigmoid(hg) * hu).astype(BF16)
        o_ref[...] = _dot(h, wd_ref[0]) * gt_ref[...]

    @pl.when(i >= na_ref[0])
    def _():
        o_ref[...] = jnp.zeros_like(o_ref)


def moe_experts(xs, gate_rows, blk_expert, n_active, w_gate, w_up, w_down):
    rows, d = xs.shape
    n_blk = rows // MOE_ROWS
    de = w_gate.shape[-1]
    grid_spec = pltpu.PrefetchScalarGridSpec(
        num_scalar_prefetch=2,
        grid=(n_blk,),
        in_specs=[pl.BlockSpec((MOE_ROWS, d), lambda i, be, na: (i, 0)),
                  pl.BlockSpec((1, d, de), lambda i, be, na: (be[i], 0, 0)),
                  pl.BlockSpec((1, d, de), lambda i, be, na: (be[i], 0, 0)),
                  pl.BlockSpec((1, de, d), lambda i, be, na: (be[i], 0, 0)),
                  pl.BlockSpec((MOE_ROWS, 1), lambda i, be, na: (i, 0))],
        out_specs=pl.BlockSpec((MOE_ROWS, d), lambda i, be, na: (i, 0)),
    )
    return pl.pallas_call(
        _moe_kernel,
        grid_spec=grid_spec,
        out_shape=jax.ShapeDtypeStruct((rows, d), F32),
        compiler_params=_params(("arbitrary",)),
        name="moe_experts",
    )(blk_expert, n_active, xs, w_gate, w_up, w_down, gate_rows)


def hierarchical_moe(dims, x, g_norm, mod, p, n_rows):
    n_grp = p["w_grp"].shape[1]
    epg = p["w_router"].shape[-1]
    n_exp = n_grp * epg
    d = dims.d
    w_rt = jnp.concatenate([p["w_grp"], p["w_router"].transpose(1, 0, 2).reshape(d, n_exp)], axis=1)
    b_rt = jnp.concatenate([p["b_grp"], p["b_router"].reshape(n_exp)])
    padc = -w_rt.shape[1] % LANES
    f, rt = nm_router(dims, x, g_norm, mod["sh2"], mod["sc2"], jnp.pad(w_rt, ((0, 0), (0, padc))),
                      jnp.pad(b_rt, (0, padc)), n_rows, n_grp, epg)

    gates = rt[:, :TOP_K]
    flat_e = rt[:, TOP_K:2 * TOP_K].astype(jnp.int32).reshape(-1)
    n_assign = n_rows * TOP_K
    i32 = jnp.int32
    order = jnp.argsort(flat_e).astype(i32)
    inv = jnp.argsort(order).astype(i32)
    e_sorted = flat_e[order]
    counts = jnp.sum(flat_e[:, None] == jnp.arange(n_exp, dtype=i32)[None, :], axis=0, dtype=i32)
    padded = (counts + MOE_ROWS - 1) // MOE_ROWS * MOE_ROWS
    seg_start = jnp.cumsum(counts) - counts
    pad_end = jnp.cumsum(padded)
    pad_start = pad_end - padded
    slot = pad_start[e_sorted] + jnp.arange(n_assign, dtype=i32) - seg_start[e_sorted]
    n_blk = -(-n_assign // MOE_ROWS) + n_exp
    blk_expert = jnp.minimum(jnp.sum(jnp.arange(n_blk, dtype=i32)[:, None] >= (pad_end // MOE_ROWS)[None, :],
                                     axis=1, dtype=i32), n_exp - 1)
    n_active = (pad_end[-1] // MOE_ROWS).astype(i32).reshape(1)
    row = jnp.arange(n_blk * MOE_ROWS, dtype=i32)
    row_e = jnp.repeat(blk_expert, MOE_ROWS)
    rank = row - pad_start[row_e]
    valid = rank < counts[row_e]
    pos = jnp.clip(seg_start[row_e] + rank, 0, n_assign - 1)
    src = jnp.where(valid, order[pos] // TOP_K, 0)
    gate_rows = jnp.where(valid, gates.reshape(-1)[order][pos], 0.0)
    slot_of = slot[inv].reshape(n_rows, TOP_K)

    xs = jnp.take(f, src, axis=0)
    out = moe_experts(xs, gate_rows[:, None], blk_expert, n_active, p["w_gate"].astype(BF16),
                      p["w_up"].astype(BF16), p["w_down"].astype(BF16))
    y0 = jnp.take(out, slot_of[:, 0], axis=0)
    y1 = jnp.take(out, slot_of[:, 1], axis=0)
    return combine_residual(dims, x, y0, y1, mod["g2"], n_rows)


def kernel(x, c, ctx, c_ctx, norm_mix_g, norm_ffn_g, ada_w, ada_b, even_w_in, rwkv_mu, rwkv_w0, rwkv_w_up, rwkv_a0, rwkv_a_up, rwkv_k_k, rwkv_k_a, rwkv_r_k, rwkv_g_up, rwkv_gn_g, rwkv_gn_b, conv_w, conv_b, conv_ln_g, conv_ln_b, even_w_out, odd_w_in, att_qn_g, att_kn_g, lru_conv_w, lru_conv_b, lru_w_rg, lru_b_rg, lru_w_ig, lru_b_ig, lru_lambda, odd_w_out, moe_w_grp, moe_b_grp, moe_w_router, moe_b_router, moe_w_gate, moe_w_up, moe_w_down, final_g):
    bsz, s, d = x.shape
    n_ctx = ctx.shape[1]
    depth = norm_mix_g.shape[0]
    hd = att_qn_g.shape[-1]
    dims = _Dims(bsz, s, n_ctx, d)
    xa = jnp.concatenate([x.reshape(bsz * s, d), ctx.reshape(bsz * n_ctx, d)], axis=0)

    cond = jnp.concatenate([c, c_ctx[None, :]], axis=0)
    cond = cond * _sigmoid(cond)
    rows = -(bsz + 1) % 8
    cond = jnp.pad(cond, ((0, rows), (0, 0)))

    for layer in range(depth):
        ctx_out = layer < depth - 1
        n_rows = dims.n if ctx_out else dims.n_lat
        j = layer // 2
        mod_all = ada_matmul(cond, ada_w[layer], ada_b[layer])[:bsz + 1]
        names = ("sh1", "sc1", "g1", "sh2", "sc2", "g2")
        mod = {nm: mod_all[:, q * d:(q + 1) * d].reshape(bsz + 1, 1, d) for q, nm in enumerate(names)}
        if layer % 2 == 0:
            p = dict(w_in=even_w_in[j], mu=rwkv_mu[j], w0=rwkv_w0[j], w_up=rwkv_w_up[j], a0=rwkv_a0[j],
                     a_up=rwkv_a_up[j], k_k=rwkv_k_k[j], k_a=rwkv_k_a[j], r_k=rwkv_r_k[j].reshape(-1),
                     g_up=rwkv_g_up[j], gn_g=rwkv_gn_g[j], gn_b=rwkv_gn_b[j], conv_w=conv_w[j],
                     conv_b=conv_b[j], ln_g=conv_ln_g[j], ln_b=conv_ln_b[j], w_out=even_w_out[j])
            x_new = even_mixer(dims, xa, norm_mix_g[layer], mod, p, hd, n_rows)
        else:
            p = dict(w_in=odd_w_in[j], qn_g=att_qn_g[j], kn_g=att_kn_g[j], conv_w=lru_conv_w[j],
                     conv_b=lru_conv_b[j], w_rg=lru_w_rg[j], b_rg=lru_b_rg[j].reshape(2, -1),
                     w_ig=lru_w_ig[j], b_ig=lru_b_ig[j].reshape(2, -1), lam=lru_lambda[j], w_out=odd_w_out[j])
            x_new = odd_mixer(dims, xa, norm_mix_g[layer], mod, p, hd, ctx_out, n_rows)
        pm = dict(w_grp=moe_w_grp[layer], b_grp=moe_b_grp[layer], w_router=moe_w_router[layer],
                  b_router=moe_b_router[layer], w_gate=moe_w_gate[layer], w_up=moe_w_up[layer],
                  w_down=moe_w_down[layer])
        xa = hierarchical_moe(dims, x_new, norm_ffn_g[layer], mod, pm, n_rows)
    return final_norm(dims, xa, final_g, dims.n_lat).reshape(bsz, s, d)
```

```python
import functools
import math

import jax
import jax.numpy as jnp
from jax import lax
from jax.experimental import pallas as pl
from jax.experimental.pallas import tpu as pltpu

F32 = jnp.float32
BF16 = jnp.bfloat16

GRID_W = 64
ROPE_THETA = 10000.0
RWKV_GN_EPS = 64e-5
LRU_C = 8.0
TOP_K = 2
RMS_EPS = 1e-6
LN_EPS = 1e-5
LANES = 128
MXU_DIM = 256
VMEM_LIMIT = 48 * 2**20
MOE_ROWS = 256
SCAN_TB = 32


def _params(sem):
    return pltpu.CompilerParams(dimension_semantics=sem, vmem_limit_bytes=VMEM_LIMIT)


def _dot(a, b):
    return jnp.dot(a, b, preferred_element_type=F32)


def _split(w):
    hi = w.astype(BF16)
    return hi, (w - hi.astype(F32)).astype(BF16)


def _dot3(a, w_hi, w_lo):
    a_hi = a.astype(BF16)
    a_lo = (a - a_hi.astype(F32)).astype(BF16)
    return _dot(a_hi, w_hi) + _dot(a_lo, w_hi) + _dot(a_hi, w_lo)


def _sigmoid(x):
    return 1.0 / (1.0 + jnp.exp(-x))


def _pow2_tile(limit, *dims):
    t = limit
    while any(d % t for d in dims):
        t //= 2
    return t


class _Dims:
    def __init__(self, b, s, c, d):
        self.b, self.s, self.c, self.d = b, s, c, d
        self.n_lat = b * s
        self.n = b * s + b * c
        self.tm = _pow2_tile(1024, s, b * c)
        self.tr = _pow2_tile(256, s, c)
        assert self.tr == c, "sequence kernels assume one context tile per batch element"

    def mod_index(self, tile):
        n_lat_tiles = self.n_lat // tile

        def f(i):
            return jnp.where(i < n_lat_tiles, (i * tile) // self.s, self.b)
        return f

    def seq_index(self, i):
        per_seq = self.s // self.tr
        n_lat_tiles = self.n_lat // self.tr
        return jnp.where(i < n_lat_tiles, (i // per_seq) * (per_seq + 1) + 1 + i % per_seq,
                         (i - n_lat_tiles) * (per_seq + 1))


def _ada_kernel(a_ref, wh_ref, wl_ref, b_ref, o_ref):
    o_ref[...] = _dot3(a_ref[...], wh_ref[...], wl_ref[...]) + b_ref[...]


def ada_matmul(a, w, bias):
    r, d = a.shape
    m = w.shape[1]
    tn = _pow2_tile(1024, m)
    wh, wl = _split(w)
    return pl.pallas_call(
        _ada_kernel,
        grid=(m // tn,),
        in_specs=[pl.BlockSpec((r, d), lambda j: (0, 0)),
                  pl.BlockSpec((d, tn), lambda j: (0, j)),
                  pl.BlockSpec((d, tn), lambda j: (0, j)),
                  pl.BlockSpec((1, tn), lambda j: (0, j))],
        out_specs=pl.BlockSpec((r, tn), lambda j: (0, j)),
        out_shape=jax.ShapeDtypeStruct((r, m), F32),
        compiler_params=_params(("arbitrary",)),
        name="ada_matmul",
    )(a, wh, wl, bias.reshape(1, m))


def _norm_mod(x_ref, g_ref, sh_ref, sc_ref):
    xf = x_ref[...]
    ms = jnp.mean(xf * xf, axis=-1, keepdims=True)
    y = xf * lax.rsqrt(ms + RMS_EPS) * g_ref[...]
    return y * (1.0 + sc_ref[0]) + sh_ref[0]


def _nm_matmul_kernel(x_ref, g_ref, sh_ref, sc_ref, w_ref, z_ref, h_scr):
    @pl.when(pl.program_id(1) == 0)
    def _():
        h_scr[...] = _norm_mod(x_ref, g_ref, sh_ref, sc_ref).astype(BF16)

    z_ref[...] = _dot(h_scr[...], w_ref[...])


def nm_matmul(dims, x, g, shift, scale, w, n_rows):
    d = dims.d
    m = w.shape[1]
    tm = dims.tm
    tn = _pow2_tile(512, m)
    midx = dims.mod_index(tm)
    return pl.pallas_call(
        _nm_matmul_kernel,
        grid=(n_rows // tm, m // tn),
        in_specs=[pl.BlockSpec((tm, d), lambda i, j: (i, 0)),
                  pl.BlockSpec((1, d), lambda i, j: (0, 0)),
                  pl.BlockSpec((1, 1, d), lambda i, j: (midx(i), 0, 0)),
                  pl.BlockSpec((1, 1, d), lambda i, j: (midx(i), 0, 0)),
                  pl.BlockSpec((d, tn), lambda i, j: (0, j))],
        out_specs=pl.BlockSpec((tm, tn), lambda i, j: (i, j)),
        out_shape=jax.ShapeDtypeStruct((n_rows, m), F32),
        scratch_shapes=[pltpu.VMEM((tm, d), BF16)],
        compiler_params=_params(("parallel", "arbitrary")),
        name="nm_matmul",
    )(x, g.reshape(1, d), shift, scale, w)


def _nm_router_kernel(x_ref, g_ref, sh_ref, sc_ref, wh_ref, wl_ref, b_ref, f_ref, rt_ref, *, n_grp, epg):
    h = _norm_mod(x_ref, g_ref, sh_ref, sc_ref)
    f_ref[...] = h.astype(BF16)
    lg = _dot3(h, wh_ref[...], wl_ref[...]) + b_ref[...]
    lane = lax.broadcasted_iota(jnp.int32, (1, lg.shape[1]), 1).astype(F32)
    neg = -jnp.inf
    big = float(lg.shape[1])

    def first_max(vals):
        mx = jnp.max(vals, axis=-1, keepdims=True)
        return mx, jnp.min(jnp.where(vals == mx, lane, big), axis=-1, keepdims=True)

    gl = jnp.where(lane < n_grp, lg, neg)
    gmax, gidx = first_max(gl)
    grp_p = 1.0 / jnp.sum(jnp.exp(gl - gmax), axis=-1, keepdims=True)
    lo = n_grp + gidx * epg
    el = jnp.where(lane >= lo, jnp.where(lane < lo + epg, lg, neg), neg)
    m1, i1 = first_max(el)
    m2, i2 = first_max(jnp.where(lane == i1, neg, el))
    p2 = jnp.exp(m2 - m1)
    g1 = grp_p / (1.0 + p2)
    g2 = grp_p * p2 / (1.0 + p2)
    rt_ref[...] = jnp.where(lane == 0, g1, jnp.where(lane == 1, g2, jnp.where(
        lane == 2, i1 - n_grp, jnp.where(lane == 3, i2 - n_grp, 0.0))))


def nm_router(dims, x, g, shift, scale, w_rt, b_rt, n_rows, n_grp, epg):
    d = dims.d
    tm = dims.tm
    m = w_rt.shape[1]
    wh, wl = _split(w_rt)
    midx = dims.mod_index(tm)
    return pl.pallas_call(
        functools.partial(_nm_router_kernel, n_grp=n_grp, epg=epg),
        grid=(n_rows // tm,),
        in_specs=[pl.BlockSpec((tm, d), lambda i: (i, 0)),
                  pl.BlockSpec((1, d), lambda i: (0, 0)),
                  pl.BlockSpec((1, 1, d), lambda i: (midx(i), 0, 0)),
                  pl.BlockSpec((1, 1, d), lambda i: (midx(i), 0, 0)),
                  pl.BlockSpec((d, m), lambda i: (0, 0)),
                  pl.BlockSpec((d, m), lambda i: (0, 0)),
                  pl.BlockSpec((1, m), lambda i: (0, 0))],
        out_specs=[pl.BlockSpec((tm, d), lambda i: (i, 0)),
                   pl.BlockSpec((tm, m), lambda i: (i, 0))],
        out_shape=[jax.ShapeDtypeStruct((n_rows, d), BF16),
                   jax.ShapeDtypeStruct((n_rows, m), F32)],
        compiler_params=_params(("parallel",)),
        name="nm_router",
    )(x, g.reshape(1, d), shift, scale, wh, wl, b_rt.reshape(1, m))


def _mm_res_kernel(a1_ref, a2_ref, w1_ref, w2_ref, x_ref, gt_ref, o_ref):
    acc = _dot(a1_ref[...], w1_ref[...]) + _dot(a2_ref[...], w2_ref[...])
    o_ref[...] = x_ref[...] + gt_ref[0] * acc


def matmul_residual(dims, a1, a2, w1, w2, x, gate, n_rows):
    d = dims.d
    tm = dims.tm
    tn = _pow2_tile(512, d)
    k1, k2 = a1.shape[1], a2.shape[1]
    midx = dims.mod_index(tm)
    out = pl.pallas_call(
        _mm_res_kernel,
        grid=(n_rows // tm, d // tn),
        in_specs=[pl.BlockSpec((tm, k1), lambda i, j: (i, 0)),
                  pl.BlockSpec((tm, k2), lambda i, j: (i, 0)),
                  pl.BlockSpec((k1, tn), lambda i, j: (0, j)),
                  pl.BlockSpec((k2, tn), lambda i, j: (0, j)),
                  pl.BlockSpec((tm, tn), lambda i, j: (i, j)),
                  pl.BlockSpec((1, 1, tn), lambda i, j: (midx(i), 0, j))],
        out_specs=pl.BlockSpec((tm, tn), lambda i, j: (i, j)),
        out_shape=jax.ShapeDtypeStruct((n_rows, d), F32),
        compiler_params=_params(("parallel", "arbitrary")),
        name="matmul_residual",
    )(a1, a2, w1, w2, x, gate)
    return out


def _combine_kernel(x_ref, y0_ref, y1_ref, gt_ref, o_ref):
    o_ref[...] = x_ref[...] + gt_ref[0] * (y0_ref[...] + y1_ref[...])


def combine_residual(dims, x, y0, y1, gate, n_rows):
    d = dims.d
    tm = _pow2_tile(512, dims.tm)
    midx = dims.mod_index(tm)
    row = pl.BlockSpec((tm, d), lambda i: (i, 0))
    return pl.pallas_call(
        _combine_kernel,
        grid=(n_rows // tm,),
        in_specs=[row, row, row, pl.BlockSpec((1, 1, d), lambda i: (midx(i), 0, 0))],
        out_specs=row,
        out_shape=jax.ShapeDtypeStruct((n_rows, d), F32),
        compiler_params=_params(("parallel",)),
        name="combine_residual",
    )(x, y0, y1, gate)


def _final_norm_kernel(x_ref, g_ref, o_ref):
    xf = x_ref[...]
    ms = jnp.mean(xf * xf, axis=-1, keepdims=True)
    o_ref[...] = xf * lax.rsqrt(ms + RMS_EPS) * g_ref[...]


def final_norm(dims, x, g, n_rows):
    d = dims.d
    tm = _pow2_tile(512, dims.tm)
    return pl.pallas_call(
        _final_norm_kernel,
        grid=(n_rows // tm,),
        in_specs=[pl.BlockSpec((tm, d), lambda i: (i, 0)), pl.BlockSpec((1, d), lambda i: (0, 0))],
        out_specs=pl.BlockSpec((tm, d), lambda i: (i, 0)),
        out_shape=jax.ShapeDtypeStruct((n_rows, d), F32),
        compiler_params=_params(("parallel",)),
        name="final_norm",
    )(x, g.reshape(1, d))


def _seg_edges(dims, i):
    tr = dims.tr
    n_lat_tiles = dims.n_lat // tr
    per_seq = dims.s // tr
    is_lat = i < n_lat_tiles
    first = jnp.where(is_lat, i % per_seq == 0, True)
    last = jnp.where(is_lat, i % per_seq == per_seq - 1, True)
    return first, last


def _block_ones(hd):
    idx = jnp.arange(LANES) // hd
    return (idx[:, None] == idx[None, :]).astype(BF16)


def _seg_sum(x, q):
    outs = []
    for c in range(x.shape[1] // LANES):
        xc = x[:, c * LANES:(c + 1) * LANES]
        hi = xc.astype(BF16)
        lo = (xc - hi.astype(F32)).astype(BF16)
        outs.append(_dot(hi, q) + _dot(lo, q))
    return jnp.concatenate(outs, axis=1) if len(outs) > 1 else outs[0]


def _neighbour(x, halo_ref, reverse, at_edge):
    rows = x.shape[0]
    rid = lax.broadcasted_iota(jnp.int32, (rows, 1), 0)
    if reverse:
        edge_row = jnp.where(at_edge, 0.0, halo_ref[0:1, :])
        return jnp.where(rid == rows - 1, edge_row, pltpu.roll(x, rows - 1, axis=0))
    edge_row = jnp.where(at_edge, 0.0, halo_ref[halo_ref.shape[0] - 1:, :])
    return jnp.where(rid == 0, edge_row, pltpu.roll(x, 1, axis=0))


def _rwkv_prep_kernel(r_ref, k_ref, v_ref, lw_ref, la_ref, rh_ref, kh_ref, vh_ref, lwh_ref, lah_ref,
                      mur_ref, muk_ref, muv_ref, mulw_ref, mula_ref, w0_ref, wuh_ref, wul_ref,
                      a0_ref, auh_ref, aul_ref, ka_ref, rk_ref, q_ref,
                      wo_ref, ao_ref, bo_ref, *, dims, reverse):
    first, last = _seg_edges(dims, pl.program_id(0))
    at_edge = last if reverse else first

    def mix(x_ref, h_ref, mu_ref):
        x = x_ref[...]
        return x + (_neighbour(x, h_ref, reverse, at_edge) - x) * mu_ref[...]

    r = mix(r_ref, rh_ref, mur_ref)
    k = mix(k_ref, kh_ref, muk_ref)
    v = mix(v_ref, vh_ref, muv_ref)
    lw = mix(lw_ref, lwh_ref, mulw_ref)
    la = mix(la_ref, lah_ref, mula_ref)
    q = q_ref[...]
    u = w0_ref[...] + _dot3(jnp.tanh(lw), wuh_ref[...], wul_ref[...])
    decay = jnp.exp(-math.exp(-0.5) * _sigmoid(u))
    a = _sigmoid(a0_ref[...] + _dot3(la, auh_ref[...], aul_ref[...]))
    kf = k * (1.0 + (a - 1.0) * ka_ref[...])
    wo_ref[...] = decay
    ao_ref[...] = a
    bo_ref[...] = _seg_sum(r * kf * rk_ref[...], q) * v


def rwkv_prep(dims, z_rkv, z_lora, p, d_idx, hd):
    reverse = d_idx == 1
    tr, n = dims.tr, dims.n
    w = p["w0"].shape[-1]
    lw_w = p["w_up"].shape[-2]
    la_w = p["a_up"].shape[-2]
    assert 2 * lw_w == LANES and 2 * la_w == LANES
    mu = p["mu"][d_idx]

    def half(vec, width):
        z = jnp.zeros_like(vec)
        return jnp.concatenate([vec, z] if d_idx == 0 else [z, vec], axis=0)

    mu_lw = half(mu[3 * w:3 * w + lw_w], lw_w).reshape(1, LANES)
    mu_la = half(mu[3 * w + lw_w:], la_w).reshape(1, LANES)
    wuh, wul = _split(half(p["w_up"][d_idx], lw_w))
    auh, aul = _split(half(p["a_up"][d_idx], la_w))
    row = lambda c: pl.BlockSpec((tr, w), lambda i: (i, c))
    chunk = lambda c: pl.BlockSpec((tr, LANES), lambda i: (i, c))

    def halo_c(c):
        per = tr // 8
        nb = n // 8
        if reverse:
            return pl.BlockSpec((8, LANES), lambda i: (jnp.minimum((i + 1) * per, nb - 1), c))
        return pl.BlockSpec((8, LANES), lambda i: (jnp.maximum(i * per - 1, 0), c))

    def halo_w(c):
        per = tr // 8
        nb = n // 8
        if reverse:
            return pl.BlockSpec((8, w), lambda i: (jnp.minimum((i + 1) * per, nb - 1), c))
        return pl.BlockSpec((8, w), lambda i: (jnp.maximum(i * per - 1, 0), c))

    vec = lambda width: pl.BlockSpec((1, width), lambda i: (0, 0))
    mat = lambda: pl.BlockSpec((LANES, w), lambda i: (0, 0))
    outs = pl.pallas_call(
        functools.partial(_rwkv_prep_kernel, dims=dims, reverse=reverse),
        grid=(n // tr,),
        in_specs=[row(0), row(1), row(2), chunk(0), chunk(1),
                  halo_w(0), halo_w(1), halo_w(2), halo_c(0), halo_c(1),
                  vec(w), vec(w), vec(w), vec(LANES), vec(LANES), vec(w), mat(), mat(),
                  vec(w), mat(), mat(), vec(w), vec(w),
                  pl.BlockSpec((LANES, LANES), lambda i: (0, 0))],
        out_specs=[pl.BlockSpec((tr, w), lambda i: (dims.seq_index(i), 0))] * 2
        + [pl.BlockSpec((tr, w), lambda i: (i, 0))],
        out_shape=[jax.ShapeDtypeStruct((n, w), F32)] * 3,
        compiler_params=_params(("parallel",)),
        name="rwkv_prep",
    )(z_rkv, z_rkv, z_rkv, z_lora, z_lora, z_rkv, z_rkv, z_rkv, z_lora, z_lora,
      mu[:w].reshape(1, w), mu[w:2 * w].reshape(1, w), mu[2 * w:3 * w].reshape(1, w), mu_lw, mu_la,
      p["w0"][d_idx].reshape(1, w), wuh, wul, p["a0"][d_idx].reshape(1, w), auh, aul,
      p["k_a"].reshape(1, w), p["r_k"].reshape(1, w), _block_ones(hd))
    return outs


def _rwkv_scan_kernel(r_ref, k_ref, v_ref, w_ref, a_ref, mur_ref, muk_ref, muv_ref, kkt_ref, kat_ref,
                      y_ref, s_ref, prev_ref, kk_ref, ka_ref, kf_ref, wr_ref, *, tb, hd, reverse, n_ctx_blocks):
    i = pl.program_id(0)

    @pl.when(i == 0)
    def _():
        s_ref[...] = jnp.zeros_like(s_ref)

    @pl.when((i == 0) | (i == n_ctx_blocks))
    def _():
        prev_ref[...] = jnp.zeros_like(prev_ref)

    def step(s, carry):
        t = tb - 1 - s if reverse else s
        r_raw, k_raw, v_raw = r_ref[t], k_ref[t], v_ref[t]
        r_t = r_raw + (prev_ref[0] - r_raw) * mur_ref[...]
        k_t = k_raw + (prev_ref[1] - k_raw) * muk_ref[...]
        v_t = v_raw + (prev_ref[2] - v_raw) * muv_ref[...]
        prev_ref[0] = r_raw
        prev_ref[1] = k_raw
        prev_ref[2] = v_raw
        a_t = a_ref[t]
        kk = k_t * kkt_ref[...]
        kk = kk * lax.rsqrt(jnp.maximum(jnp.sum(kk * kk, axis=0, keepdims=True), 1e-24))
        ka = kk * a_t
        kf = k_t * (1.0 + (a_t - 1.0) * kat_ref[...])
        kk_ref[...] = kk
        ka_ref[...] = ka
        kf_ref[...] = kf
        wr_ref[...] = w_ref[t] * r_t
        c1 = jnp.sum(ka * r_t, axis=0, keepdims=True)
        c2 = jnp.sum(kf * r_t, axis=0, keepdims=True)
        acc_sa = jnp.zeros_like(v_t)
        acc_u = jnp.zeros_like(v_t)
        for kx in range(hd):
            sk = s_ref[kx]
            acc_sa = acc_sa + sk * kk_ref[pl.ds(kx, 1), :]
            acc_u = acc_u + sk * wr_ref[pl.ds(kx, 1), :]
        sa = -acc_sa
        y_ref[t] = acc_u + sa * c1 + v_t * c2
        for kx in range(hd):
            s_ref[kx] = (s_ref[kx] * w_ref[t, pl.ds(kx, 1), :] + sa * ka_ref[pl.ds(kx, 1), :]
                         + v_t * kf_ref[pl.ds(kx, 1), :])
        return carry

    lax.fori_loop(0, tb, step, 0)


def rwkv_scan(rkv, w, a, tabs, n_ctx, reverse):
    t_all, hd, lanes = w.shape
    tb = _pow2_tile(SCAN_TB, n_ctx, t_all - n_ctx)
    n_blocks = t_all // tb
    n_ctx_blocks = n_ctx // tb

    def tmap(i):
        if not reverse:
            return (i, 0, 0)
        return (jnp.where(i < n_ctx_blocks, n_ctx_blocks - 1 - i, n_blocks - 1 - (i - n_ctx_blocks)), 0, 0)

    spec = pl.BlockSpec((tb, hd, lanes), tmap)
    tab = pl.BlockSpec((hd, lanes), lambda i: (0, 0))
    tile = pltpu.VMEM((hd, lanes), F32)
    return pl.pallas_call(
        functools.partial(_rwkv_scan_kernel, tb=tb, hd=hd, reverse=reverse, n_ctx_blocks=n_ctx_blocks),
        grid=(n_blocks,),
        in_specs=[spec] * 5 + [tab] * 5,
        out_specs=spec,
        out_shape=jax.ShapeDtypeStruct(w.shape, F32),
        scratch_shapes=[pltpu.VMEM((hd, hd, lanes), F32), pltpu.VMEM((3, hd, lanes), F32), tile, tile, tile, tile],
        compiler_params=_params(("arbitrary",)),
        name="rwkv_scan",
    )(*rkv, w, a, *tabs)


def _rwkv_gn_kernel(y0_ref, y1_ref, g_ref, b_ref, o_ref):
    y = y0_ref[...] + y1_ref[...]
    mean = jnp.mean(y, axis=1, keepdims=True)
    yc = y - mean
    var = jnp.mean(yc * yc, axis=1, keepdims=True)
    o_ref[...] = yc * lax.rsqrt(var + RWKV_GN_EPS) * g_ref[...] + b_ref[...]


def rwkv_group_norm(y0, y1, g_tab, b_tab):
    t_all, hd, lanes = y0.shape
    tb = _pow2_tile(64, t_all)
    blk = pl.BlockSpec((tb, hd, lanes), lambda i: (i, 0, 0))
    tab = pl.BlockSpec((1, hd, lanes), lambda i: (0, 0, 0))
    return pl.pallas_call(
        _rwkv_gn_kernel,
        grid=(t_all // tb,),
        in_specs=[blk, blk, tab, tab],
        out_specs=blk,
        out_shape=jax.ShapeDtypeStruct((t_all, hd, lanes), F32),
        compiler_params=_params(("parallel",)),
        name="rwkv_group_norm",
    )(y0, y1, g_tab, b_tab)


def _even_merge_kernel(yn_ref, b0_ref, b1_ref, zg_ref, val_ref, gate_ref, valp_ref, gatep_ref, valn_ref,
                       gaten_ref, guh_ref, gul_ref, cw_ref, cb_ref, lng_ref, lnb_ref,
                       a_ref, bo_ref, u_scr, *, dims, taps, halo):
    first, last = _seg_edges(dims, pl.program_id(0))
    tr = dims.tr
    g = _dot3(_sigmoid(zg_ref[...]), guh_ref[...], gul_ref[...])
    a_ref[...] = ((yn_ref[...] + b0_ref[...] + b1_ref[...]) * g).astype(BF16)

    glu = lambda v_ref, g_ref: v_ref[...] * _sigmoid(g_ref[...])
    u_scr[0:halo, :] = jnp.where(first, 0.0, glu(valp_ref, gatep_ref))
    u_scr[halo:halo + tr, :] = glu(val_ref, gate_ref)
    u_scr[halo + tr:, :] = jnp.where(last, 0.0, glu(valn_ref, gaten_ref))
    pad = taps // 2
    acc = jnp.zeros((tr, u_scr.shape[1]), F32) + cb_ref[...]
    for j in range(taps):
        acc = acc + cw_ref[j:j + 1, :] * u_scr[pl.ds(halo - pad + j, tr), :]
    mu = jnp.mean(acc, axis=-1, keepdims=True)
    xc = acc - mu
    var = jnp.mean(xc * xc, axis=-1, keepdims=True)
    y = xc * lax.rsqrt(var + LN_EPS) * lng_ref[...] + lnb_ref[...]
    bo_ref[...] = (y * _sigmoid(y)).astype(BF16)


def even_merge(dims, yn, bonus0, bonus1, z_lora, z_cv, p):
    tr, n = dims.tr, dims.n
    w = yn.shape[1]
    taps, cw = p["conv_w"].shape
    lg = p["g_up"].shape[0]
    assert lg == LANES
    halo = 16
    assert taps // 2 <= halo and tr % halo == 0
    guh, gul = _split(p["g_up"])
    per = tr // halo
    nb = n // halo
    row_w = pl.BlockSpec((tr, w), lambda i: (i, 0))
    col = lambda c: pl.BlockSpec((tr, cw), lambda i: (i, c))
    prev = lambda c: pl.BlockSpec((halo, cw), lambda i: (jnp.maximum(i * per - 1, 0), c))
    nxt = lambda c: pl.BlockSpec((halo, cw), lambda i: (jnp.minimum((i + 1) * per, nb - 1), c))
    vec = lambda width: pl.BlockSpec((1, width), lambda i: (0, 0))
    return pl.pallas_call(
        functools.partial(_even_merge_kernel, dims=dims, taps=taps, halo=halo),
        grid=(n // tr,),
        in_specs=[pl.BlockSpec((tr, w), lambda i: (dims.seq_index(i), 0)), row_w, row_w,
                  pl.BlockSpec((tr, LANES), lambda i: (i, 2)),
                  col(0), col(1), prev(0), prev(1), nxt(0), nxt(1),
                  pl.BlockSpec((lg, w), lambda i: (0, 0)), pl.BlockSpec((lg, w), lambda i: (0, 0)),
                  pl.BlockSpec((taps, cw), lambda i: (0, 0)), vec(cw), vec(cw), vec(cw)],
        out_specs=[pl.BlockSpec((tr, w), lambda i: (i, 0)), pl.BlockSpec((tr, cw), lambda i: (i, 0))],
        out_shape=[jax.ShapeDtypeStruct((n, w), BF16), jax.ShapeDtypeStruct((n, cw), BF16)],
        scratch_shapes=[pltpu.VMEM((tr + 2 * halo, cw), F32)],
        compiler_params=_params(("parallel",)),
        name="even_merge",
    )(yn, bonus0, bonus1, z_lora, z_cv, z_cv, z_cv, z_cv, z_cv, z_cv, guh, gul,
      p["conv_w"], p["conv_b"].reshape(1, cw), p["ln_g"].reshape(1, cw), p["ln_b"].reshape(1, cw))


def even_mixer(dims, x, g_norm, mod, p, hd, n_out_rows):
    b, s, c, n = dims.b, dims.s, dims.c, dims.n
    w = p["w0"].shape[-1]
    cw = p["conv_w"].shape[1]
    lw_w, la_w, lg = p["w_up"].shape[-2], p["a_up"].shape[-2], p["g_up"].shape[0]
    w_in = p["w_in"]
    o = 3 * w
    lora_cols = w_in[:, o:o + 2 * lw_w + 2 * la_w + lg]
    lora_pad = jnp.pad(lora_cols, ((0, 0), (0, -lora_cols.shape[1] % (4 * LANES))))
    rwkv_in = o + 2 * lw_w + 2 * la_w + lg
    z_rkv = nm_matmul(dims, x, g_norm, mod["sh1"], mod["sc1"], w_in[:, :o].astype(BF16), n)
    z_lora = nm_matmul(dims, x, g_norm, mod["sh1"], mod["sc1"], lora_pad.astype(BF16), n)
    z_cv = nm_matmul(dims, x, g_norm, mod["sh1"], mod["sc1"], w_in[:, rwkv_in:].astype(BF16), n)

    heads = w // hd
    t_all = c + s

    def to_scan(a, t_len):
        return a.reshape(b, t_len, heads, hd).transpose(1, 3, 0, 2).reshape(t_len, hd, b * heads)

    def head_tab(vec):
        return jnp.tile(vec.reshape(heads, hd).T[:, None, :], (1, b, 1)).reshape(hd, b * heads)

    rkv = [jnp.concatenate([to_scan(z_rkv[dims.n_lat:, q * w:(q + 1) * w], c),
                            to_scan(z_rkv[:dims.n_lat, q * w:(q + 1) * w], s)], axis=0) for q in range(3)]
    per_dir = [rwkv_prep(dims, z_rkv, z_lora, p, d_idx, hd) for d_idx in range(2)]
    ys = []
    for d_idx in range(2):
        mu = p["mu"][d_idx]
        tabs = [head_tab(mu[q * w:(q + 1) * w]) for q in range(3)] + [head_tab(p["k_k"]), head_tab(p["k_a"])]
        ys.append(rwkv_scan(rkv, to_scan(per_dir[d_idx][0], t_all), to_scan(per_dir[d_idx][1], t_all), tabs,
                            c, d_idx == 1))

    yn = rwkv_group_norm(ys[0], ys[1], head_tab(p["gn_g"])[None], head_tab(p["gn_b"])[None])
    yn = yn.reshape(t_all, hd, b, heads).transpose(2, 0, 3, 1).reshape(b * t_all, w)
    a_out, b_out = even_merge(dims, yn, per_dir[0][2], per_dir[1][2], z_lora, z_cv, p)
    w_out = p["w_out"].astype(BF16)
    return matmul_residual(dims, a_out, b_out, w_out[:w], w_out[w:], x, mod["g1"], n_out_rows)


def _qkv_prep_kernel(zq_ref, zk_ref, zv_ref, cos_ref, sin_ref, gq_ref, gk_ref, qm_ref,
                     q_ref, k_ref, v_ref, *, hd):
    qm = qm_ref[...]
    cos = cos_ref[...]
    sin = sin_ref[...]
    lane = lax.broadcasted_iota(jnp.int32, (1, LANES), 1)
    low = (lane % hd) < hd // 2

    def norm_rope(z, gain, scale):
        ms = _seg_sum(z * z, qm) * (1.0 / hd)
        y = z * lax.rsqrt(ms + RMS_EPS) * gain
        outs = []
        for c in range(y.shape[1] // LANES):
            yc = y[:, c * LANES:(c + 1) * LANES]
            partner = jnp.where(low, pltpu.roll(yc, LANES - hd // 2, axis=1), pltpu.roll(yc, hd // 2, axis=1))
            outs.append((yc * cos + partner * sin) * scale)
        return jnp.concatenate(outs, axis=1) if len(outs) > 1 else outs[0]

    q_ref[...] = norm_rope(zq_ref[...], gq_ref[...], hd ** -0.5 * math.log2(math.e)).astype(BF16)
    k_ref[...] = norm_rope(zk_ref[...], gk_ref[...], 1.0).astype(BF16)
    v_ref[...] = zv_ref[...].astype(BF16)


def qkv_prep(dims, z_main, z_kv, qn_g, kn_g, aw, kw, hd):
    tr, n, s = dims.tr, dims.n, dims.s
    half = hd // 2
    pairs = half // 2
    t = jnp.arange(s)
    inv = ROPE_THETA ** (-jnp.arange(pairs, dtype=F32) / pairs)
    ang = jnp.concatenate([(t // GRID_W).astype(F32)[:, None] * inv, (t % GRID_W).astype(F32)[:, None] * inv], -1)
    cos, sin = jnp.cos(ang), jnp.sin(ang)
    reps = LANES // hd
    cos_t = jnp.tile(jnp.concatenate([cos, cos], -1), (1, reps))
    sin_t = jnp.tile(jnp.concatenate([-sin, sin], -1), (1, reps))
    cos_t = jnp.concatenate([cos_t, jnp.ones((tr, LANES), F32)], axis=0)
    sin_t = jnp.concatenate([sin_t, jnp.zeros((tr, LANES), F32)], axis=0)
    per_seq = s // tr
    n_lat_tiles = dims.n_lat // tr
    tab = pl.BlockSpec((tr, LANES), lambda i: (jnp.where(i < n_lat_tiles, i % per_seq, per_seq), 0))
    assert kw % LANES == 0 and aw % kw == 0
    return pl.pallas_call(
        functools.partial(_qkv_prep_kernel, hd=hd),
        grid=(n // tr,),
        in_specs=[pl.BlockSpec((tr, aw), lambda i: (i, 0)),
                  pl.BlockSpec((tr, kw), lambda i: (i, 0)),
                  pl.BlockSpec((tr, kw), lambda i: (i, 1)),
                  tab, tab,
                  pl.BlockSpec((1, aw), lambda i: (0, 0)), pl.BlockSpec((1, kw), lambda i: (0, 0)),
                  pl.BlockSpec((LANES, LANES), lambda i: (0, 0))],
        out_specs=[pl.BlockSpec((tr, aw), lambda i: (i, 0)), pl.BlockSpec((tr, kw), lambda i: (i, 0)),
                   pl.BlockSpec((tr, kw), lambda i: (i, 0))],
        out_shape=[jax.ShapeDtypeStruct((n, aw), BF16), jax.ShapeDtypeStruct((n, kw), BF16),
                   jax.ShapeDtypeStruct((n, kw), BF16)],
        compiler_params=_params(("parallel",)),
        name="qkv_prep",
    )(z_main, z_kv, z_kv, cos_t, sin_t, jnp.tile(qn_g, aw // hd).reshape(1, aw),
      jnp.tile(kn_g, kw // hd).reshape(1, kw), _block_ones(hd))


def _flash_kernel(*refs, kvh, grp, hd, tq, tk, n_lat_chunks):
    if n_lat_chunks:
        qt_ref, kc_ref, vtc_ref, kl_ref, vtl_ref, o_ref, qz_scr, m_scr, l_scr, acc_scr = refs
    else:
        qt_ref, kc_ref, vtc_ref, o_ref, qz_scr, m_scr, l_scr, acc_scr = refs
    qz_scr[...] = jnp.zeros_like(qz_scr)
    for g in range(kvh):
        for u in range(grp):
            h = g * grp + u
            qz_scr[g, g * hd:(g + 1) * hd, u * tq:(u + 1) * tq] = qt_ref[h * hd:(h + 1) * hd, :]
    m_scr[...] = jnp.full_like(m_scr, -jnp.inf)
    l_scr[...] = jnp.zeros_like(l_scr)
    acc_scr[...] = jnp.zeros_like(acc_scr)

    def chunk(k, vt_of):
        for g in range(kvh):
            st = _dot(k, qz_scr[g])
            m_prev = m_scr[g]
            m_new = jnp.maximum(m_prev, jnp.max(st, axis=0, keepdims=True))
            alpha = jnp.exp2(m_prev - m_new)
            pt = jnp.exp2(st - m_new)
            l_scr[g] = alpha * l_scr[g] + jnp.sum(pt, axis=0, keepdims=True)
            acc_scr[g] = alpha * acc_scr[g] + _dot(vt_of(g), pt.astype(BF16))
            m_scr[g] = m_new

    chunk(kc_ref[...], lambda g: vtc_ref[0, g * hd:(g + 1) * hd, :])
    if n_lat_chunks:
        def body(j, carry):
            off = pl.multiple_of(j * tk, tk)
            chunk(kl_ref[pl.ds(off, tk), :], lambda g: vtl_ref[j, g * hd:(g + 1) * hd, :])
            return carry

        lax.fori_loop(0, n_lat_chunks, body, 0)

    for g in range(kvh):
        out = acc_scr[g] * (1.0 / l_scr[g])
        for u in range(grp):
            h = g * grp + u
            o_ref[h * hd:(h + 1) * hd, :] = out[:, u * tq:(u + 1) * tq].astype(BF16)


def flash_attention(dims, qt, k, vt_ctx, vt_lat, hd, q_tile0, n_q_tiles, n_lat_chunks):
    aw, kw = qt.shape[0], k.shape[1]
    tq = tk = dims.tr
    kvh = kw // hd
    grp = aw // kw
    ctx0 = dims.n_lat // dims.c
    in_specs = [pl.BlockSpec((aw, tq), lambda bi, qi: (0, q_tile0 + bi * n_q_tiles + qi)),
                pl.BlockSpec((dims.c, kw), lambda bi, qi: (ctx0 + bi, 0)),
                pl.BlockSpec((1, kw, dims.c), lambda bi, qi: (bi, 0, 0))]
    args = [qt, k, vt_ctx]
    if n_lat_chunks:
        in_specs += [pl.BlockSpec((dims.s, kw), lambda bi, qi: (bi, 0)),
                     pl.BlockSpec((n_lat_chunks, kw, tk), lambda bi, qi: (bi, 0, 0))]
        args += [k, vt_lat]
    return pl.pallas_call(
        functools.partial(_flash_kernel, kvh=kvh, grp=grp, hd=hd, tq=tq, tk=tk, n_lat_chunks=n_lat_chunks),
        grid=(dims.b, n_q_tiles),
        in_specs=in_specs,
        out_specs=pl.BlockSpec((aw, tq), lambda bi, qi: (0, bi * n_q_tiles + qi)),
        out_shape=jax.ShapeDtypeStruct((aw, dims.b * n_q_tiles * tq), BF16),
        scratch_shapes=[pltpu.VMEM((kvh, kw, grp * tq), BF16), pltpu.VMEM((kvh, 1, grp * tq), F32),
                        pltpu.VMEM((kvh, 1, grp * tq), F32), pltpu.VMEM((kvh, hd, grp * tq), F32)],
        compiler_params=_params(("parallel", "parallel")),
        name="flash_attention",
    )(*args)


def attention(dims, q, k, v, hd, ctx_out):
    b, s, c, tr = dims.b, dims.s, dims.c, dims.tr
    kw = k.shape[1]
    nl = dims.n_lat
    qt = q.T
    vt_ctx = v[nl:].reshape(b, c, kw).transpose(0, 2, 1)
    vt_lat = v[:nl].reshape(b * (s // tr), tr, kw).transpose(0, 2, 1)
    att = flash_attention(dims, qt, k, vt_ctx, vt_lat, hd, 0, s // tr, s // tr)
    if not ctx_out:
        return att.T
    att_c = flash_attention(dims, qt, k, vt_ctx, None, hd, nl // tr, c // tr, 0)
    return jnp.concatenate([att, att_c], axis=1).T


def _lru_prep_kernel(x_ref, xh_ref, cw_ref, cb_ref, wrh_ref, wrl_ref, brg_ref, wih_ref, wil_ref, big_ref,
                     lam_ref, a_ref, b_ref, x_scr, *, dims, reverse, taps):
    first, last = _seg_edges(dims, pl.program_id(0))
    tr = dims.tr
    if reverse:
        x_scr[0:tr, :] = x_ref[...]
        x_scr[tr:, :] = jnp.where(last, 0.0, xh_ref[...])
        base = 0
    else:
        x_scr[0:8, :] = jnp.where(first, 0.0, xh_ref[...])
        x_scr[8:, :] = x_ref[...]
        base = 8 - (taps - 1)
    xc = jnp.zeros((tr, x_scr.shape[1]), F32) + cb_ref[...]
    for j in range(taps):
        xc = xc + cw_ref[j:j + 1, :] * x_scr[pl.ds(base + j, tr), :]
    cwd = wrh_ref.shape[1]
    rg, ig = [], []
    for c in range(wrh_ref.shape[0]):
        xcc = xc[:, c * cwd:(c + 1) * cwd]
        rg.append(_dot3(xcc, wrh_ref[c], wrl_ref[c]))
        ig.append(_dot3(xcc, wih_ref[c], wil_ref[c]))
    cat = lambda xs: jnp.concatenate(xs, axis=1) if len(xs) > 1 else xs[0]
    rg = _sigmoid(cat(rg) + brg_ref[...])
    ig = _sigmoid(cat(ig) + big_ref[...])
    nl = -lam_ref[...]
    softplus = jnp.maximum(nl, 0.0) + jnp.log1p(jnp.exp(-jnp.abs(nl)))
    log_a = -LRU_C * rg * softplus
    a = jnp.exp(log_a)
    a_ref[...] = a
    b_ref[...] = jnp.sqrt(1.0 - a * a) * (ig * xc)


def _block_diag(wb, cwd):
    nb, bw, _ = wb.shape
    per = cwd // bw
    wb = wb.reshape(nb // per, per, bw, bw)
    eye = jnp.eye(per, dtype=wb.dtype)
    return jnp.einsum("cpij,pq->cpiqj", wb, eye).reshape(nb // per, cwd, cwd)


def lru_prep(dims, z_main, col, p, d_idx):
    reverse = d_idx == 1
    tr, n = dims.tr, dims.n
    taps, lw = p["conv_w"].shape[-2:]
    cwd = min(MXU_DIM, lw)
    wrh, wrl = _split(_block_diag(p["w_rg"][d_idx], cwd))
    wih, wil = _split(_block_diag(p["w_ig"][d_idx], cwd))
    per = tr // 8
    nb = n // 8
    if reverse:
        halo = pl.BlockSpec((8, lw), lambda i: (jnp.minimum((i + 1) * per, nb - 1), col))
    else:
        halo = pl.BlockSpec((8, lw), lambda i: (jnp.maximum(i * per - 1, 0), col))
    vec = pl.BlockSpec((1, lw), lambda i: (0, 0))
    mat = pl.BlockSpec((lw // cwd, cwd, cwd), lambda i: (0, 0, 0))
    return pl.pallas_call(
        functools.partial(_lru_prep_kernel, dims=dims, reverse=reverse, taps=taps),
        grid=(n // tr,),
        in_specs=[pl.BlockSpec((tr, lw), lambda i: (i, col)), halo,
                  pl.BlockSpec((taps, lw), lambda i: (0, 0)), vec, mat, mat, vec, mat, mat, vec, vec],
        out_specs=[pl.BlockSpec((tr, lw), lambda i: (i, 0))] * 2,
        out_shape=[jax.ShapeDtypeStruct((n, lw), F32)] * 2,
        scratch_shapes=[pltpu.VMEM((tr + 8, lw), F32)],
        compiler_params=_params(("parallel",)),
        name="lru_prep",
    )(z_main, z_main, p["conv_w"][d_idx], p["conv_b"][d_idx].reshape(1, lw), wrh, wrl,
      p["b_rg"][d_idx].reshape(1, lw), wih, wil, p["b_ig"][d_idx].reshape(1, lw),
      p["lam"][d_idx].reshape(1, lw))


def _lru_scan_kernel(a_ref, b_ref, h_ref, h_scr, *, tr):
    d = pl.program_id(0)

    @pl.when(pl.program_id(2) == 0)
    def _():
        h_scr[...] = jnp.zeros_like(h_scr)

    def step(s, h):
        t = jnp.where(d == 0, s, tr - 1 - s)
        h = a_ref[0, pl.ds(t, 1), :] * h + b_ref[0, pl.ds(t, 1), :]
        h_ref[0, pl.ds(t, 1), :] = h
        return h

    h_scr[...] = lax.fori_loop(0, tr, step, h_scr[...])


def lru_scan(dims, a, b):
    tr, s, bsz = dims.tr, dims.s, dims.b
    lw = a.shape[-1]
    lat_tiles = s // tr
    ctx0 = dims.n_lat // tr

    def tmap(d, bi, j):
        lat = jnp.where(d == 0, j - 1, lat_tiles - j)
        return (d, jnp.where(j == 0, ctx0 + bi, bi * lat_tiles + lat), 0)

    spec = pl.BlockSpec((1, tr, lw), tmap)
    return pl.pallas_call(
        functools.partial(_lru_scan_kernel, tr=tr),
        grid=(2, bsz, lat_tiles + 1),
        in_specs=[spec, spec],
        out_specs=spec,
        out_shape=jax.ShapeDtypeStruct(a.shape, F32),
        scratch_shapes=[pltpu.VMEM((1, lw), F32)],
        compiler_params=_params(("arbitrary", "arbitrary", "arbitrary")),
        name="lru_scan",
    )(a, b)


def _lru_gate_kernel(h_ref, g_ref, o_ref):
    g = g_ref[...]
    gelu = 0.5 * g * (1.0 + jnp.tanh(math.sqrt(2.0 / math.pi) * (g + 0.044715 * (g * g * g))))
    o_ref[...] = ((h_ref[0] + h_ref[1]) * gelu).astype(BF16)


def lru_gate(dims, h, z_main, col):
    tr, n = dims.tr, dims.n
    lw = h.shape[-1]
    return pl.pallas_call(
        _lru_gate_kernel,
        grid=(n // tr,),
        in_specs=[pl.BlockSpec((2, tr, lw), lambda i: (0, i, 0)), pl.BlockSpec((tr, lw), lambda i: (i, col))],
        out_specs=pl.BlockSpec((tr, lw), lambda i: (i, 0)),
        out_shape=jax.ShapeDtypeStruct((n, lw), BF16),
        compiler_params=_params(("parallel",)),
        name="lru_gate",
    )(h, z_main)


def odd_mixer(dims, x, g_norm, mod, p, hd, ctx_out, n_out_rows):
    n = dims.n
    lw = p["lam"].shape[-1]
    w_in = p["w_in"]
    aw = p["w_out"].shape[0] - lw
    kw = (w_in.shape[1] - aw - 2 * lw) // 2
    assert aw == lw
    main_cols = jnp.concatenate([w_in[:, :aw], w_in[:, aw + 2 * kw:]], axis=1)
    z_main = nm_matmul(dims, x, g_norm, mod["sh1"], mod["sc1"], main_cols.astype(BF16), n)
    z_kv = nm_matmul(dims, x, g_norm, mod["sh1"], mod["sc1"], w_in[:, aw:aw + 2 * kw].astype(BF16), n)
    q, k, v = qkv_prep(dims, z_main, z_kv, p["qn_g"], p["kn_g"], aw, kw, hd)
    att = attention(dims, q, k, v, hd, ctx_out)
    ab = [lru_prep(dims, z_main, 1, p, d_idx) for d_idx in range(2)]
    h = lru_scan(dims, jnp.stack([ab[0][0], ab[1][0]]), jnp.stack([ab[0][1], ab[1][1]]))
    hg = lru_gate(dims, h, z_main, 2)
    w_out = p["w_out"].astype(BF16)
    return matmul_residual(dims, att, hg, w_out[:aw], w_out[aw:], x, mod["g1"], n_out_rows)


def _moe_kernel(be_ref, na_ref, x_ref, wg_ref, wu_ref, wd_ref, gt_ref, o_ref):
    i = pl.program_id(0)

    @pl.when(i < na_ref[0])
    def _():
        x = x_ref[...]
        hg = _dot(x, wg_ref[0])
        hu = _dot(x, wu_ref[0])
        h = (hg * _sigmoid(hg) * hu).astype(BF16)
        o_ref[...] = _dot(h, wd_ref[0]) * gt_ref[...]

    @pl.when(i >= na_ref[0])
    def _():
        o_ref[...] = jnp.zeros_like(o_ref)


def moe_experts(xs, gate_rows, blk_expert, n_active, w_gate, w_up, w_down):
    rows, d = xs.shape
    n_blk = rows // MOE_ROWS
    de = w_gate.shape[-1]
    grid_spec = pltpu.PrefetchScalarGridSpec(
        num_scalar_prefetch=2,
        grid=(n_blk,),
        in_specs=[pl.BlockSpec((MOE_ROWS, d), lambda i, be, na: (i, 0)),
                  pl.BlockSpec((1, d, de), lambda i, be, na: (be[i], 0, 0)),
                  pl.BlockSpec((1, d, de), lambda i, be, na: (be[i], 0, 0)),
                  pl.BlockSpec((1, de, d), lambda i, be, na: (be[i], 0, 0)),
                  pl.BlockSpec((MOE_ROWS, 1), lambda i, be, na: (i, 0))],
        out_specs=pl.BlockSpec((MOE_ROWS, d), lambda i, be, na: (i, 0)),
    )
    return pl.pallas_call(
        _moe_kernel,
        grid_spec=grid_spec,
        out_shape=jax.ShapeDtypeStruct((rows, d), F32),
        compiler_params=_params(("arbitrary",)),
        name="moe_experts",
    )(blk_expert, n_active, xs, w_gate, w_up, w_down, gate_rows)


def hierarchical_moe(dims, x, g_norm, mod, p, n_rows):
    n_grp = p["w_grp"].shape[1]
    epg = p["w_router"].shape[-1]
    n_exp = n_grp * epg
    d = dims.d
    w_rt = jnp.concatenate([p["w_grp"], p["w_router"].transpose(1, 0, 2).reshape(d, n_exp)], axis=1)
    b_rt = jnp.concatenate([p["b_grp"], p["b_router"].reshape(n_exp)])
    padc = -w_rt.shape[1] % LANES
    f, rt = nm_router(dims, x, g_norm, mod["sh2"], mod["sc2"], jnp.pad(w_rt, ((0, 0), (0, padc))),
                      jnp.pad(b_rt, (0, padc)), n_rows, n_grp, epg)

    gates = rt[:, :TOP_K]
    flat_e = rt[:, TOP_K:2 * TOP_K].astype(jnp.int32).reshape(-1)
    n_assign = n_rows * TOP_K
    i32 = jnp.int32
    order = jnp.argsort(flat_e).astype(i32)
    inv = jnp.argsort(order).astype(i32)
    e_sorted = flat_e[order]
    counts = jnp.sum(flat_e[:, None] == jnp.arange(n_exp, dtype=i32)[None, :], axis=0, dtype=i32)
    padded = (counts + MOE_ROWS - 1) // MOE_ROWS * MOE_ROWS
    seg_start = jnp.cumsum(counts) - counts
    pad_end = jnp.cumsum(padded)
    pad_start = pad_end - padded
    slot = pad_start[e_sorted] + jnp.arange(n_assign, dtype=i32) - seg_start[e_sorted]
    n_blk = -(-n_assign // MOE_ROWS) + n_exp
    blk_expert = jnp.minimum(jnp.sum(jnp.arange(n_blk, dtype=i32)[:, None] >= (pad_end // MOE_ROWS)[None, :],
                                     axis=1, dtype=i32), n_exp - 1)
    n_active = (pad_end[-1] // MOE_ROWS).astype(i32).reshape(1)
    rank = (jnp.arange(n_blk, dtype=i32) * MOE_ROWS - pad_start[blk_expert])[:, None] \
        + jnp.arange(MOE_ROWS, dtype=i32)[None, :]
    valid = (rank < counts[blk_expert][:, None]).reshape(-1)
    pos = jnp.clip(seg_start[blk_expert][:, None] + rank, 0, n_assign - 1).reshape(-1)
    src = jnp.where(valid, order[pos] // TOP_K, 0)
    gate_rows = jnp.where(valid, gates.reshape(-1)[order][pos], 0.0)
    slot_of = slot[inv].reshape(n_rows, TOP_K)

    xs = jnp.take(f, src, axis=0, mode="clip")
    out = moe_experts(xs, gate_rows[:, None], blk_expert, n_active, p["w_gate"].astype(BF16),
                      p["w_up"].astype(BF16), p["w_down"].astype(BF16))
    y0 = jnp.take(out, slot_of[:, 0], axis=0, mode="clip")
    y1 = jnp.take(out, slot_of[:, 1], axis=0, mode="clip")
    return combine_residual(dims, x, y0, y1, mod["g2"], n_rows)


def kernel(x, c, ctx, c_ctx, norm_mix_g, norm_ffn_g, ada_w, ada_b, even_w_in, rwkv_mu, rwkv_w0, rwkv_w_up, rwkv_a0, rwkv_a_up, rwkv_k_k, rwkv_k_a, rwkv_r_k, rwkv_g_up, rwkv_gn_g, rwkv_gn_b, conv_w, conv_b, conv_ln_g, conv_ln_b, even_w_out, odd_w_in, att_qn_g, att_kn_g, lru_conv_w, lru_conv_b, lru_w_rg, lru_b_rg, lru_w_ig, lru_b_ig, lru_lambda, odd_w_out, moe_w_grp, moe_b_grp, moe_w_router, moe_b_router, moe_w_gate, moe_w_up, moe_w_down, final_g):
    bsz, s, d = x.shape
    n_ctx = ctx.shape[1]
    depth = norm_mix_g.shape[0]
    hd = att_qn_g.shape[-1]
    dims = _Dims(bsz, s, n_ctx, d)
    xa = jnp.concatenate([x.reshape(bsz * s, d), ctx.reshape(bsz * n_ctx, d)], axis=0)

    cond = jnp.concatenate([c, c_ctx[None, :]], axis=0)
    cond = cond * _sigmoid(cond)
    rows = -(bsz + 1) % 8
    cond = jnp.pad(cond, ((0, rows), (0, 0)))

    for layer in range(depth):
        ctx_out = layer < depth - 1
        n_rows = dims.n if ctx_out else dims.n_lat
        j = layer // 2
        mod_all = ada_matmul(cond, ada_w[layer], ada_b[layer])[:bsz + 1]
        names = ("sh1", "sc1", "g1", "sh2", "sc2", "g2")
        mod = {nm: mod_all[:, q * d:(q + 1) * d].reshape(bsz + 1, 1, d) for q, nm in enumerate(names)}
        if layer % 2 == 0:
            p = dict(w_in=even_w_in[j], mu=rwkv_mu[j], w0=rwkv_w0[j], w_up=rwkv_w_up[j], a0=rwkv_a0[j],
                     a_up=rwkv_a_up[j], k_k=rwkv_k_k[j], k_a=rwkv_k_a[j], r_k=rwkv_r_k[j].reshape(-1),
                     g_up=rwkv_g_up[j], gn_g=rwkv_gn_g[j], gn_b=rwkv_gn_b[j], conv_w=conv_w[j],
                     conv_b=conv_b[j], ln_g=conv_ln_g[j], ln_b=conv_ln_b[j], w_out=even_w_out[j])
            x_new = even_mixer(dims, xa, norm_mix_g[layer], mod, p, hd, n_rows)
        else:
            p = dict(w_in=odd_w_in[j], qn_g=att_qn_g[j], kn_g=att_kn_g[j], conv_w=lru_conv_w[j],
                     conv_b=lru_conv_b[j], w_rg=lru_w_rg[j], b_rg=lru_b_rg[j].reshape(2, -1),
                     w_ig=lru_w_ig[j], b_ig=lru_b_ig[j].reshape(2, -1), lam=lru_lambda[j], w_out=odd_w_out[j])
            x_new = odd_mixer(dims, xa, norm_mix_g[layer], mod, p, hd, ctx_out, n_rows)
        pm = dict(w_grp=moe_w_grp[layer], b_grp=moe_b_grp[layer], w_router=moe_w_router[layer],
                  b_router=moe_b_router[layer], w_gate=moe_w_gate[layer], w_up=moe_w_up[layer],
                  w_down=moe_w_down[layer])
        xa = hierarchical_moe(dims, x_new, norm_ffn_g[layer], mod, pm, n_rows)
    return final_norm(dims, xa, final_g, dims.n_lat).reshape(bsz, s, d)
```

```python
import functools
import math

import jax
import jax.numpy as jnp
from jax import lax
from jax.experimental import pallas as pl
from jax.experimental.pallas import tpu as pltpu

F32 = jnp.float32
BF16 = jnp.bfloat16

GRID_W = 64
ROPE_THETA = 10000.0
RWKV_GN_EPS = 64e-5
LRU_C = 8.0
TOP_K = 2
RMS_EPS = 1e-6
LN_EPS = 1e-5
LANES = 128
MXU_DIM = 256
VMEM_LIMIT = 48 * 2**20
MOE_ROWS = 256
SCAN_TB = 32


def _params(sem):
    return pltpu.CompilerParams(dimension_semantics=sem, vmem_limit_bytes=VMEM_LIMIT)


def _dot(a, b):
    return jnp.dot(a, b, preferred_element_type=F32)


def _split(w):
    hi = w.astype(BF16)
    return hi, (w - hi.astype(F32)).astype(BF16)


def _dot3(a, w_hi, w_lo):
    a_hi = a.astype(BF16)
    a_lo = (a - a_hi.astype(F32)).astype(BF16)
    return _dot(a_hi, w_hi) + _dot(a_lo, w_hi) + _dot(a_hi, w_lo)


def _sigmoid(x):
    return 1.0 / (1.0 + jnp.exp(-x))


def _pow2_tile(limit, *dims):
    t = limit
    while any(d % t for d in dims):
        t //= 2
    return t


class _Dims:
    def __init__(self, b, s, c, d):
        self.b, self.s, self.c, self.d = b, s, c, d
        self.n_lat = b * s
        self.n = b * s + b * c
        self.tm = _pow2_tile(1024, s, b * c)
        self.tr = _pow2_tile(256, s, c)
        assert self.tr == c, "sequence kernels assume one context tile per batch element"

    def mod_index(self, tile):
        n_lat_tiles = self.n_lat // tile

        def f(i):
            return jnp.where(i < n_lat_tiles, (i * tile) // self.s, self.b)
        return f

    def seq_index(self, i):
        per_seq = self.s // self.tr
        n_lat_tiles = self.n_lat // self.tr
        return jnp.where(i < n_lat_tiles, (i // per_seq) * (per_seq + 1) + 1 + i % per_seq,
                         (i - n_lat_tiles) * (per_seq + 1))


def _ada_kernel(a_ref, wh_ref, wl_ref, b_ref, o_ref):
    o_ref[...] = _dot3(a_ref[...], wh_ref[...], wl_ref[...]) + b_ref[...]


def ada_matmul(a, w, bias):
    r, d = a.shape
    m = w.shape[1]
    tn = _pow2_tile(1024, m)
    wh, wl = _split(w)
    return pl.pallas_call(
        _ada_kernel,
        grid=(m // tn,),
        in_specs=[pl.BlockSpec((r, d), lambda j: (0, 0)),
                  pl.BlockSpec((d, tn), lambda j: (0, j)),
                  pl.BlockSpec((d, tn), lambda j: (0, j)),
                  pl.BlockSpec((1, tn), lambda j: (0, j))],
        out_specs=pl.BlockSpec((r, tn), lambda j: (0, j)),
        out_shape=jax.ShapeDtypeStruct((r, m), F32),
        compiler_params=_params(("arbitrary",)),
        name="ada_matmul",
    )(a, wh, wl, bias.reshape(1, m))


def _norm_mod(x_ref, g_ref, sh_ref, sc_ref):
    xf = x_ref[...]
    ms = jnp.mean(xf * xf, axis=-1, keepdims=True)
    y = xf * lax.rsqrt(ms + RMS_EPS) * g_ref[...]
    return y * (1.0 + sc_ref[0]) + sh_ref[0]


def _nm_matmul_kernel(x_ref, g_ref, sh_ref, sc_ref, w_ref, z_ref, h_scr):
    @pl.when(pl.program_id(1) == 0)
    def _():
        h_scr[...] = _norm_mod(x_ref, g_ref, sh_ref, sc_ref).astype(BF16)

    z_ref[...] = _dot(h_scr[...], w_ref[...])


def nm_matmul(dims, x, g, shift, scale, w, n_rows):
    d = dims.d
    m = w.shape[1]
    tm = dims.tm
    tn = _pow2_tile(1024, m)
    midx = dims.mod_index(tm)
    return pl.pallas_call(
        _nm_matmul_kernel,
        grid=(n_rows // tm, m // tn),
        in_specs=[pl.BlockSpec((tm, d), lambda i, j: (i, 0)),
                  pl.BlockSpec((1, d), lambda i, j: (0, 0)),
                  pl.BlockSpec((1, 1, d), lambda i, j: (midx(i), 0, 0)),
                  pl.BlockSpec((1, 1, d), lambda i, j: (midx(i), 0, 0)),
                  pl.BlockSpec((d, tn), lambda i, j: (0, j))],
        out_specs=pl.BlockSpec((tm, tn), lambda i, j: (i, j)),
        out_shape=jax.ShapeDtypeStruct((n_rows, m), F32),
        scratch_shapes=[pltpu.VMEM((tm, d), BF16)],
        compiler_params=_params(("parallel", "arbitrary")),
        name="nm_matmul",
    )(x, g.reshape(1, d), shift, scale, w)


def _nm_router_kernel(x_ref, g_ref, sh_ref, sc_ref, wh_ref, wl_ref, b_ref, f_ref, rt_ref, *, n_grp, epg):
    h = _norm_mod(x_ref, g_ref, sh_ref, sc_ref)
    f_ref[...] = h.astype(BF16)
    lg = _dot3(h, wh_ref[...], wl_ref[...]) + b_ref[...]
    lane = lax.broadcasted_iota(jnp.int32, (1, lg.shape[1]), 1).astype(F32)
    neg = -jnp.inf
    big = float(lg.shape[1])

    def first_max(vals):
        mx = jnp.max(vals, axis=-1, keepdims=True)
        return mx, jnp.min(jnp.where(vals == mx, lane, big), axis=-1, keepdims=True)

    gl = jnp.where(lane < n_grp, lg, neg)
    gmax, gidx = first_max(gl)
    grp_p = 1.0 / jnp.sum(jnp.exp(gl - gmax), axis=-1, keepdims=True)
    lo = n_grp + gidx * epg
    el = jnp.where(lane >= lo, jnp.where(lane < lo + epg, lg, neg), neg)
    m1, i1 = first_max(el)
    m2, i2 = first_max(jnp.where(lane == i1, neg, el))
    p2 = jnp.exp(m2 - m1)
    g1 = grp_p / (1.0 + p2)
    g2 = grp_p * p2 / (1.0 + p2)
    rt_ref[...] = jnp.where(lane == 0, g1, jnp.where(lane == 1, g2, jnp.where(
        lane == 2, i1 - n_grp, jnp.where(lane == 3, i2 - n_grp, 0.0))))


def nm_router(dims, x, g, shift, scale, w_rt, b_rt, n_rows, n_grp, epg):
    d = dims.d
    tm = dims.tm
    m = w_rt.shape[1]
    wh, wl = _split(w_rt)
    midx = dims.mod_index(tm)
    return pl.pallas_call(
        functools.partial(_nm_router_kernel, n_grp=n_grp, epg=epg),
        grid=(n_rows // tm,),
        in_specs=[pl.BlockSpec((tm, d), lambda i: (i, 0)),
                  pl.BlockSpec((1, d), lambda i: (0, 0)),
                  pl.BlockSpec((1, 1, d), lambda i: (midx(i), 0, 0)),
                  pl.BlockSpec((1, 1, d), lambda i: (midx(i), 0, 0)),
                  pl.BlockSpec((d, m), lambda i: (0, 0)),
                  pl.BlockSpec((d, m), lambda i: (0, 0)),
                  pl.BlockSpec((1, m), lambda i: (0, 0))],
        out_specs=[pl.BlockSpec((tm, d), lambda i: (i, 0)),
                   pl.BlockSpec((tm, m), lambda i: (i, 0))],
        out_shape=[jax.ShapeDtypeStruct((n_rows, d), BF16),
                   jax.ShapeDtypeStruct((n_rows, m), F32)],
        compiler_params=_params(("parallel",)),
        name="nm_router",
    )(x, g.reshape(1, d), shift, scale, wh, wl, b_rt.reshape(1, m))


def _mm_res_kernel(a1_ref, a2_ref, w1_ref, w2_ref, x_ref, gt_ref, o_ref, *, a1_feature_major):
    if a1_feature_major:
        acc = lax.dot_general(a1_ref[...], w1_ref[...], (((0,), (0,)), ((), ())), preferred_element_type=F32)
    else:
        acc = _dot(a1_ref[...], w1_ref[...])
    acc = acc + _dot(a2_ref[...], w2_ref[...])
    o_ref[...] = x_ref[...] + gt_ref[0] * acc


def matmul_residual(dims, a1, a2, w1, w2, x, gate, n_rows, a1_feature_major=False):
    d = dims.d
    tm = dims.tm
    tn = _pow2_tile(512, d)
    k1, k2 = w1.shape[0], a2.shape[1]
    midx = dims.mod_index(tm)
    a1_spec = pl.BlockSpec((k1, tm), lambda i, j: (0, i)) if a1_feature_major else \
        pl.BlockSpec((tm, k1), lambda i, j: (i, 0))
    out = pl.pallas_call(
        functools.partial(_mm_res_kernel, a1_feature_major=a1_feature_major),
        grid=(n_rows // tm, d // tn),
        in_specs=[a1_spec,
                  pl.BlockSpec((tm, k2), lambda i, j: (i, 0)),
                  pl.BlockSpec((k1, tn), lambda i, j: (0, j)),
                  pl.BlockSpec((k2, tn), lambda i, j: (0, j)),
                  pl.BlockSpec((tm, tn), lambda i, j: (i, j)),
                  pl.BlockSpec((1, 1, tn), lambda i, j: (midx(i), 0, j))],
        out_specs=pl.BlockSpec((tm, tn), lambda i, j: (i, j)),
        out_shape=jax.ShapeDtypeStruct((n_rows, d), F32),
        compiler_params=_params(("parallel", "arbitrary")),
        name="matmul_residual",
    )(a1, a2, w1, w2, x, gate)
    return out


def _combine_kernel(x_ref, y0_ref, y1_ref, gt_ref, o_ref):
    o_ref[...] = x_ref[...] + gt_ref[0] * (y0_ref[...] + y1_ref[...])


def combine_residual(dims, x, y0, y1, gate, n_rows):
    d = dims.d
    tm = _pow2_tile(512, dims.tm)
    midx = dims.mod_index(tm)
    row = pl.BlockSpec((tm, d), lambda i: (i, 0))
    return pl.pallas_call(
        _combine_kernel,
        grid=(n_rows // tm,),
        in_specs=[row, row, row, pl.BlockSpec((1, 1, d), lambda i: (midx(i), 0, 0))],
        out_specs=row,
        out_shape=jax.ShapeDtypeStruct((n_rows, d), F32),
        compiler_params=_params(("parallel",)),
        name="combine_residual",
    )(x, y0, y1, gate)


def _final_norm_kernel(x_ref, g_ref, o_ref):
    xf = x_ref[...]
    ms = jnp.mean(xf * xf, axis=-1, keepdims=True)
    o_ref[...] = xf * lax.rsqrt(ms + RMS_EPS) * g_ref[...]


def final_norm(dims, x, g, n_rows):
    d = dims.d
    tm = _pow2_tile(512, dims.tm)
    return pl.pallas_call(
        _final_norm_kernel,
        grid=(n_rows // tm,),
        in_specs=[pl.BlockSpec((tm, d), lambda i: (i, 0)), pl.BlockSpec((1, d), lambda i: (0, 0))],
        out_specs=pl.BlockSpec((tm, d), lambda i: (i, 0)),
        out_shape=jax.ShapeDtypeStruct((n_rows, d), F32),
        compiler_params=_params(("parallel",)),
        name="final_norm",
    )(x, g.reshape(1, d))


def _seg_edges(dims, i):
    tr = dims.tr
    n_lat_tiles = dims.n_lat // tr
    per_seq = dims.s // tr
    is_lat = i < n_lat_tiles
    first = jnp.where(is_lat, i % per_seq == 0, True)
    last = jnp.where(is_lat, i % per_seq == per_seq - 1, True)
    return first, last


def _block_ones(hd):
    idx = jnp.arange(LANES) // hd
    return (idx[:, None] == idx[None, :]).astype(BF16)


def _seg_sum(x, q):
    outs = []
    for c in range(x.shape[1] // LANES):
        xc = x[:, c * LANES:(c + 1) * LANES]
        hi = xc.astype(BF16)
        lo = (xc - hi.astype(F32)).astype(BF16)
        outs.append(_dot(hi, q) + _dot(lo, q))
    return jnp.concatenate(outs, axis=1) if len(outs) > 1 else outs[0]


def _neighbour(x, halo_ref, reverse, at_edge):
    rows = x.shape[0]
    rid = lax.broadcasted_iota(jnp.int32, (rows, 1), 0)
    if reverse:
        edge_row = jnp.where(at_edge, 0.0, halo_ref[0:1, :])
        return jnp.where(rid == rows - 1, edge_row, pltpu.roll(x, rows - 1, axis=0))
    edge_row = jnp.where(at_edge, 0.0, halo_ref[halo_ref.shape[0] - 1:, :])
    return jnp.where(rid == 0, edge_row, pltpu.roll(x, 1, axis=0))


def _rwkv_prep_kernel(r_ref, k_ref, v_ref, lw_ref, la_ref, rh_ref, kh_ref, vh_ref, lwh_ref, lah_ref,
                      mur_ref, muk_ref, muv_ref, mulw_ref, mula_ref, w0_ref, wuh_ref, wul_ref,
                      a0_ref, auh_ref, aul_ref, ka_ref, rk_ref, q_ref,
                      wo_ref, ao_ref, bo_ref, *, dims, reverse):
    first, last = _seg_edges(dims, pl.program_id(0))
    at_edge = last if reverse else first

    def mix(x_ref, h_ref, mu_ref):
        x = x_ref[...]
        return x + (_neighbour(x, h_ref, reverse, at_edge) - x) * mu_ref[...]

    r = mix(r_ref, rh_ref, mur_ref)
    k = mix(k_ref, kh_ref, muk_ref)
    v = mix(v_ref, vh_ref, muv_ref)
    lw = mix(lw_ref, lwh_ref, mulw_ref)
    la = mix(la_ref, lah_ref, mula_ref)
    q = q_ref[...]
    u = w0_ref[...] + _dot3(jnp.tanh(lw), wuh_ref[...], wul_ref[...])
    decay = jnp.exp(-math.exp(-0.5) * _sigmoid(u))
    a = _sigmoid(a0_ref[...] + _dot3(la, auh_ref[...], aul_ref[...]))
    kf = k * (1.0 + (a - 1.0) * ka_ref[...])
    wo_ref[...] = decay
    ao_ref[...] = a
    bo_ref[...] = _seg_sum(r * kf * rk_ref[...], q) * v


def rwkv_prep(dims, z_rkv, z_lora, p, d_idx, hd):
    reverse = d_idx == 1
    tr, n = dims.tr, dims.n
    w = p["w0"].shape[-1]
    lw_w = p["w_up"].shape[-2]
    la_w = p["a_up"].shape[-2]
    assert 2 * lw_w == LANES and 2 * la_w == LANES
    mu = p["mu"][d_idx]

    def half(vec, width):
        z = jnp.zeros_like(vec)
        return jnp.concatenate([vec, z] if d_idx == 0 else [z, vec], axis=0)

    mu_lw = half(mu[3 * w:3 * w + lw_w], lw_w).reshape(1, LANES)
    mu_la = half(mu[3 * w + lw_w:], la_w).reshape(1, LANES)
    wuh, wul = _split(half(p["w_up"][d_idx], lw_w))
    auh, aul = _split(half(p["a_up"][d_idx], la_w))
    row = lambda c: pl.BlockSpec((tr, w), lambda i: (i, c))
    chunk = lambda c: pl.BlockSpec((tr, LANES), lambda i: (i, c))

    def halo_c(c):
        per = tr // 8
        nb = n // 8
        if reverse:
            return pl.BlockSpec((8, LANES), lambda i: (jnp.minimum((i + 1) * per, nb - 1), c))
        return pl.BlockSpec((8, LANES), lambda i: (jnp.maximum(i * per - 1, 0), c))

    def halo_w(c):
        per = tr // 8
        nb = n // 8
        if reverse:
            return pl.BlockSpec((8, w), lambda i: (jnp.minimum((i + 1) * per, nb - 1), c))
        return pl.BlockSpec((8, w), lambda i: (jnp.maximum(i * per - 1, 0), c))

    vec = lambda width: pl.BlockSpec((1, width), lambda i: (0, 0))
    mat = lambda: pl.BlockSpec((LANES, w), lambda i: (0, 0))
    outs = pl.pallas_call(
        functools.partial(_rwkv_prep_kernel, dims=dims, reverse=reverse),
        grid=(n // tr,),
        in_specs=[row(0), row(1), row(2), chunk(0), chunk(1),
                  halo_w(0), halo_w(1), halo_w(2), halo_c(0), halo_c(1),
                  vec(w), vec(w), vec(w), vec(LANES), vec(LANES), vec(w), mat(), mat(),
                  vec(w), mat(), mat(), vec(w), vec(w),
                  pl.BlockSpec((LANES, LANES), lambda i: (0, 0))],
        out_specs=[pl.BlockSpec((tr, w), lambda i: (dims.seq_index(i), 0))] * 2
        + [pl.BlockSpec((tr, w), lambda i: (i, 0))],
        out_shape=[jax.ShapeDtypeStruct((n, w), F32)] * 3,
        compiler_params=_params(("parallel",)),
        name="rwkv_prep",
    )(z_rkv, z_rkv, z_rkv, z_lora, z_lora, z_rkv, z_rkv, z_rkv, z_lora, z_lora,
      mu[:w].reshape(1, w), mu[w:2 * w].reshape(1, w), mu[2 * w:3 * w].reshape(1, w), mu_lw, mu_la,
      p["w0"][d_idx].reshape(1, w), wuh, wul, p["a0"][d_idx].reshape(1, w), auh, aul,
      p["k_a"].reshape(1, w), p["r_k"].reshape(1, w), _block_ones(hd))
    return outs


def _rwkv_scan_kernel(rc_ref, kc_ref, vc_ref, rl_ref, kl_ref, vl_ref, w_ref, a_ref, mur_ref, muk_ref, muv_ref,
                      kkt_ref, kat_ref, y_ref, s_ref, prev_ref, kk_ref, ka_ref, kf_ref, wr_ref, p_ref,
                      *, tb, hd, reverse, n_ctx_blocks):
    i = pl.program_id(0)
    in_ctx = i < n_ctx_blocks

    @pl.when(i == 0)
    def _():
        s_ref[...] = jnp.zeros_like(s_ref)

    @pl.when((i == 0) | (i == n_ctx_blocks))
    def _():
        prev_ref[...] = jnp.zeros_like(prev_ref)

    def step(s, carry):
        t = tb - 1 - s if reverse else s
        r_raw, k_raw, v_raw = lax.cond(in_ctx, lambda: (rc_ref[t], kc_ref[t], vc_ref[t]),
                                       lambda: (rl_ref[t], kl_ref[t], vl_ref[t]))
        r_t = r_raw + (prev_ref[0] - r_raw) * mur_ref[...]
        k_t = k_raw + (prev_ref[1] - k_raw) * muk_ref[...]
        v_t = v_raw + (prev_ref[2] - v_raw) * muv_ref[...]
        prev_ref[0] = r_raw
        prev_ref[1] = k_raw
        prev_ref[2] = v_raw
        a_t = a_ref[t]
        kk = k_t * kkt_ref[...]
        kk = kk * lax.rsqrt(jnp.maximum(jnp.sum(kk * kk, axis=0, keepdims=True), 1e-24))
        ka = kk * a_t
        kf = k_t * (1.0 + (a_t - 1.0) * kat_ref[...])
        p_prev = p_ref[...]
        p_new = p_prev * w_ref[t]
        p_ref[...] = p_new
        p_inv = 1.0 / p_new
        kk_ref[...] = kk * p_prev
        ka_ref[...] = ka * p_inv
        kf_ref[...] = kf * p_inv
        wr_ref[...] = r_t * p_new
        c1 = jnp.sum(ka * r_t, axis=0, keepdims=True)
        c2 = jnp.sum(kf * r_t, axis=0, keepdims=True)
        acc_sa = jnp.zeros_like(v_t)
        acc_u = jnp.zeros_like(v_t)
        for kx in range(hd):
            sk = s_ref[kx]
            acc_sa = acc_sa + sk * kk_ref[pl.ds(kx, 1), :]
            acc_u = acc_u + sk * wr_ref[pl.ds(kx, 1), :]
        sa = -acc_sa
        y_ref[t] = acc_u + sa * c1 + v_t * c2
        for kx in range(hd):
            s_ref[kx] = s_ref[kx] + sa * ka_ref[pl.ds(kx, 1), :] + v_t * kf_ref[pl.ds(kx, 1), :]
        return carry

    p_ref[...] = jnp.ones_like(p_ref)
    lax.fori_loop(0, tb, step, 0)
    for kx in range(hd):
        s_ref[kx] = s_ref[kx] * p_ref[pl.ds(kx, 1), :]


def rwkv_scan(rkv_ctx, rkv_lat, w, a, tabs, reverse):
    t_all, hd, lanes = w.shape
    n_ctx = rkv_ctx[0].shape[0]
    tb = _pow2_tile(SCAN_TB, n_ctx, t_all - n_ctx)
    n_blocks = t_all // tb
    n_ctx_blocks = n_ctx // tb
    n_lat_blocks = n_blocks - n_ctx_blocks

    def ctx_blk(i):
        j = jnp.minimum(i, n_ctx_blocks - 1)
        return n_ctx_blocks - 1 - j if reverse else j

    def lat_blk(i):
        j = jnp.maximum(i - n_ctx_blocks, 0)
        return n_lat_blocks - 1 - j if reverse else j

    def seq_blk(i):
        return jnp.where(i < n_ctx_blocks, ctx_blk(i), n_ctx_blocks + lat_blk(i))

    blk = lambda f: pl.BlockSpec((tb, hd, lanes), lambda i: (f(i), 0, 0))
    tab = pl.BlockSpec((hd, lanes), lambda i: (0, 0))
    tile = pltpu.VMEM((hd, lanes), F32)
    return pl.pallas_call(
        functools.partial(_rwkv_scan_kernel, tb=tb, hd=hd, reverse=reverse, n_ctx_blocks=n_ctx_blocks),
        grid=(n_blocks,),
        in_specs=[blk(ctx_blk)] * 3 + [blk(lat_blk)] * 3 + [blk(seq_blk)] * 2 + [tab] * 5,
        out_specs=blk(seq_blk),
        out_shape=jax.ShapeDtypeStruct(w.shape, F32),
        scratch_shapes=[pltpu.VMEM((hd, hd, lanes), F32), pltpu.VMEM((3, hd, lanes), F32)] + [tile] * 5,
        compiler_params=_params(("arbitrary",)),
        name="rwkv_scan",
    )(*rkv_ctx, *rkv_lat, w, a, *tabs)


def _rwkv_gn_kernel(y0_ref, y1_ref, g_ref, b_ref, o_ref):
    y = y0_ref[...] + y1_ref[...]
    mean = jnp.mean(y, axis=1, keepdims=True)
    yc = y - mean
    var = jnp.mean(yc * yc, axis=1, keepdims=True)
    o_ref[...] = yc * lax.rsqrt(var + RWKV_GN_EPS) * g_ref[...] + b_ref[...]


def rwkv_group_norm(y0, y1, g_tab, b_tab):
    t_all, hd, lanes = y0.shape
    tb = _pow2_tile(64, t_all)
    blk = pl.BlockSpec((tb, hd, lanes), lambda i: (i, 0, 0))
    tab = pl.BlockSpec((1, hd, lanes), lambda i: (0, 0, 0))
    return pl.pallas_call(
        _rwkv_gn_kernel,
        grid=(t_all // tb,),
        in_specs=[blk, blk, tab, tab],
        out_specs=blk,
        out_shape=jax.ShapeDtypeStruct((t_all, hd, lanes), F32),
        compiler_params=_params(("parallel",)),
        name="rwkv_group_norm",
    )(y0, y1, g_tab, b_tab)


def _even_merge_kernel(yn_ref, b0_ref, b1_ref, zg_ref, val_ref, gate_ref, valp_ref, gatep_ref, valn_ref,
                       gaten_ref, guh_ref, gul_ref, cw_ref, cb_ref, lng_ref, lnb_ref,
                       a_ref, bo_ref, u_scr, *, dims, taps, halo):
    first, last = _seg_edges(dims, pl.program_id(0))
    tr = dims.tr
    g = _dot3(_sigmoid(zg_ref[...]), guh_ref[...], gul_ref[...])
    a_ref[...] = ((yn_ref[...] + b0_ref[...] + b1_ref[...]) * g).astype(BF16)

    glu = lambda v_ref, g_ref: v_ref[...] * _sigmoid(g_ref[...])
    u_scr[0:halo, :] = jnp.where(first, 0.0, glu(valp_ref, gatep_ref))
    u_scr[halo:halo + tr, :] = glu(val_ref, gate_ref)
    u_scr[halo + tr:, :] = jnp.where(last, 0.0, glu(valn_ref, gaten_ref))
    pad = taps // 2
    acc = jnp.zeros((tr, u_scr.shape[1]), F32) + cb_ref[...]
    for j in range(taps):
        acc = acc + cw_ref[j:j + 1, :] * u_scr[pl.ds(halo - pad + j, tr), :]
    mu = jnp.mean(acc, axis=-1, keepdims=True)
    xc = acc - mu
    var = jnp.mean(xc * xc, axis=-1, keepdims=True)
    y = xc * lax.rsqrt(var + LN_EPS) * lng_ref[...] + lnb_ref[...]
    bo_ref[...] = (y * _sigmoid(y)).astype(BF16)


def even_merge(dims, yn, bonus0, bonus1, z_lora, z_cv, p):
    tr, n = dims.tr, dims.n
    w = yn.shape[1]
    taps, cw = p["conv_w"].shape
    lg = p["g_up"].shape[0]
    assert lg == LANES
    halo = 16
    assert taps // 2 <= halo and tr % halo == 0
    guh, gul = _split(p["g_up"])
    per = tr // halo
    nb = n // halo
    row_w = pl.BlockSpec((tr, w), lambda i: (i, 0))
    col = lambda c: pl.BlockSpec((tr, cw), lambda i: (i, c))
    prev = lambda c: pl.BlockSpec((halo, cw), lambda i: (jnp.maximum(i * per - 1, 0), c))
    nxt = lambda c: pl.BlockSpec((halo, cw), lambda i: (jnp.minimum((i + 1) * per, nb - 1), c))
    vec = lambda width: pl.BlockSpec((1, width), lambda i: (0, 0))
    return pl.pallas_call(
        functools.partial(_even_merge_kernel, dims=dims, taps=taps, halo=halo),
        grid=(n // tr,),
        in_specs=[pl.BlockSpec((tr, w), lambda i: (dims.seq_index(i), 0)), row_w, row_w,
                  pl.BlockSpec((tr, LANES), lambda i: (i, 2)),
                  col(0), col(1), prev(0), prev(1), nxt(0), nxt(1),
                  pl.BlockSpec((lg, w), lambda i: (0, 0)), pl.BlockSpec((lg, w), lambda i: (0, 0)),
                  pl.BlockSpec((taps, cw), lambda i: (0, 0)), vec(cw), vec(cw), vec(cw)],
        out_specs=[pl.BlockSpec((tr, w), lambda i: (i, 0)), pl.BlockSpec((tr, cw), lambda i: (i, 0))],
        out_shape=[jax.ShapeDtypeStruct((n, w), BF16), jax.ShapeDtypeStruct((n, cw), BF16)],
        scratch_shapes=[pltpu.VMEM((tr + 2 * halo, cw), F32)],
        compiler_params=_params(("parallel",)),
        name="even_merge",
    )(yn, bonus0, bonus1, z_lora, z_cv, z_cv, z_cv, z_cv, z_cv, z_cv, guh, gul,
      p["conv_w"], p["conv_b"].reshape(1, cw), p["ln_g"].reshape(1, cw), p["ln_b"].reshape(1, cw))


def even_mixer(dims, x, g_norm, mod, p, hd, n_out_rows):
    b, s, c, n = dims.b, dims.s, dims.c, dims.n
    w = p["w0"].shape[-1]
    cw = p["conv_w"].shape[1]
    lw_w, la_w, lg = p["w_up"].shape[-2], p["a_up"].shape[-2], p["g_up"].shape[0]
    w_in = p["w_in"]
    o = 3 * w
    lora_cols = w_in[:, o:o + 2 * lw_w + 2 * la_w + lg]
    lora_pad = jnp.pad(lora_cols, ((0, 0), (0, -lora_cols.shape[1] % (4 * LANES))))
    rwkv_in = o + 2 * lw_w + 2 * la_w + lg
    z_rkv = nm_matmul(dims, x, g_norm, mod["sh1"], mod["sc1"], w_in[:, :o].astype(BF16), n)
    z_lora = nm_matmul(dims, x, g_norm, mod["sh1"], mod["sc1"], lora_pad.astype(BF16), n)
    z_cv = nm_matmul(dims, x, g_norm, mod["sh1"], mod["sc1"], w_in[:, rwkv_in:].astype(BF16), n)

    heads = w // hd
    t_all = c + s

    def to_scan(a, t_len):
        return a.reshape(b, t_len, heads, hd).transpose(1, 3, 0, 2).reshape(t_len, hd, b * heads)

    def head_tab(vec):
        return jnp.tile(vec.reshape(heads, hd).T[:, None, :], (1, b, 1)).reshape(hd, b * heads)

    rkv_ctx = [to_scan(z_rkv[dims.n_lat:, q * w:(q + 1) * w], c) for q in range(3)]
    rkv_lat = [to_scan(z_rkv[:dims.n_lat, q * w:(q + 1) * w], s) for q in range(3)]
    per_dir = [rwkv_prep(dims, z_rkv, z_lora, p, d_idx, hd) for d_idx in range(2)]
    ys = []
    for d_idx in range(2):
        mu = p["mu"][d_idx]
        tabs = [head_tab(mu[q * w:(q + 1) * w]) for q in range(3)] + [head_tab(p["k_k"]), head_tab(p["k_a"])]
        ys.append(rwkv_scan(rkv_ctx, rkv_lat, to_scan(per_dir[d_idx][0], t_all),
                            to_scan(per_dir[d_idx][1], t_all), tabs, d_idx == 1))

    yn = rwkv_group_norm(ys[0], ys[1], head_tab(p["gn_g"])[None], head_tab(p["gn_b"])[None])
    yn = yn.reshape(t_all, hd, b, heads).transpose(2, 0, 3, 1).reshape(b * t_all, w)
    a_out, b_out = even_merge(dims, yn, per_dir[0][2], per_dir[1][2], z_lora, z_cv, p)
    w_out = p["w_out"].astype(BF16)
    return matmul_residual(dims, a_out, b_out, w_out[:w], w_out[w:], x, mod["g1"], n_out_rows)


def _qkv_prep_kernel(zq_ref, zk_ref, zv_ref, cos_ref, sin_ref, gq_ref, gk_ref, qm_ref,
                     q_ref, k_ref, v_ref, *, hd):
    qm = qm_ref[...]
    cos = cos_ref[...]
    sin = sin_ref[...]
    lane = lax.broadcasted_iota(jnp.int32, (1, LANES), 1)
    low = (lane % hd) < hd // 2

    def norm_rope(z, gain, scale):
        ms = _seg_sum(z * z, qm) * (1.0 / hd)
        y = z * lax.rsqrt(ms + RMS_EPS) * gain
        outs = []
        for c in range(y.shape[1] // LANES):
            yc = y[:, c * LANES:(c + 1) * LANES]
            partner = jnp.where(low, pltpu.roll(yc, LANES - hd // 2, axis=1), pltpu.roll(yc, hd // 2, axis=1))
            outs.append((yc * cos + partner * sin) * scale)
        return jnp.concatenate(outs, axis=1) if len(outs) > 1 else outs[0]

    q_ref[...] = norm_rope(zq_ref[...], gq_ref[...], hd ** -0.5 * math.log2(math.e)).T.astype(BF16)
    k_ref[...] = norm_rope(zk_ref[...], gk_ref[...], 1.0).astype(BF16)
    v_ref[...] = zv_ref[...].astype(BF16)


def qkv_prep(dims, z_main, z_kv, qn_g, kn_g, aw, kw, hd):
    tr, n, s = dims.tr, dims.n, dims.s
    half = hd // 2
    pairs = half // 2
    t = jnp.arange(s)
    inv = ROPE_THETA ** (-jnp.arange(pairs, dtype=F32) / pairs)
    ang = jnp.concatenate([(t // GRID_W).astype(F32)[:, None] * inv, (t % GRID_W).astype(F32)[:, None] * inv], -1)
    cos, sin = jnp.cos(ang), jnp.sin(ang)
    reps = LANES // hd
    cos_t = jnp.tile(jnp.concatenate([cos, cos], -1), (1, reps))
    sin_t = jnp.tile(jnp.concatenate([-sin, sin], -1), (1, reps))
    cos_t = jnp.concatenate([cos_t, jnp.ones((tr, LANES), F32)], axis=0)
    sin_t = jnp.concatenate([sin_t, jnp.zeros((tr, LANES), F32)], axis=0)
    per_seq = s // tr
    n_lat_tiles = dims.n_lat // tr
    tab = pl.BlockSpec((tr, LANES), lambda i: (jnp.where(i < n_lat_tiles, i % per_seq, per_seq), 0))
    assert kw % LANES == 0 and aw % kw == 0
    return pl.pallas_call(
        functools.partial(_qkv_prep_kernel, hd=hd),
        grid=(n // tr,),
        in_specs=[pl.BlockSpec((tr, aw), lambda i: (i, 0)),
                  pl.BlockSpec((tr, kw), lambda i: (i, 0)),
                  pl.BlockSpec((tr, kw), lambda i: (i, 1)),
                  tab, tab,
                  pl.BlockSpec((1, aw), lambda i: (0, 0)), pl.BlockSpec((1, kw), lambda i: (0, 0)),
                  pl.BlockSpec((LANES, LANES), lambda i: (0, 0))],
        out_specs=[pl.BlockSpec((aw, tr), lambda i: (0, i)), pl.BlockSpec((tr, kw), lambda i: (i, 0)),
                   pl.BlockSpec((tr, kw), lambda i: (i, 0))],
        out_shape=[jax.ShapeDtypeStruct((aw, n), BF16), jax.ShapeDtypeStruct((n, kw), BF16),
                   jax.ShapeDtypeStruct((n, kw), BF16)],
        compiler_params=_params(("parallel",)),
        name="qkv_prep",
    )(z_main, z_kv, z_kv, cos_t, sin_t, jnp.tile(qn_g, aw // hd).reshape(1, aw),
      jnp.tile(kn_g, kw // hd).reshape(1, kw), _block_ones(hd))


def _flash_kernel(*refs, kvh, grp, hd, tq, tk, n_lat_chunks):
    if n_lat_chunks:
        qt_ref, kc_ref, vtc_ref, kl_ref, vtl_ref, o_ref, qz_scr, m_scr, l_scr, acc_scr = refs
    else:
        qt_ref, kc_ref, vtc_ref, o_ref, qz_scr, m_scr, l_scr, acc_scr = refs
    qz_scr[...] = jnp.zeros_like(qz_scr)
    for g in range(kvh):
        for u in range(grp):
            h = g * grp + u
            qz_scr[g, g * hd:(g + 1) * hd, u * tq:(u + 1) * tq] = qt_ref[h * hd:(h + 1) * hd, :]
    m_scr[...] = jnp.full_like(m_scr, -jnp.inf)
    l_scr[...] = jnp.zeros_like(l_scr)
    acc_scr[...] = jnp.zeros_like(acc_scr)

    def chunk(k, vt_of):
        sts = [_dot(k, qz_scr[g]) for g in range(kvh)]
        pts = []
        for g in range(kvh):
            m_prev = m_scr[g]
            m_new = jnp.maximum(m_prev, jnp.max(sts[g], axis=0, keepdims=True))
            alpha = jnp.exp2(m_prev - m_new)
            pt = jnp.exp2(sts[g] - m_new)
            l_scr[g] = alpha * l_scr[g] + jnp.sum(pt, axis=0, keepdims=True)
            acc_scr[g] = alpha * acc_scr[g]
            m_scr[g] = m_new
            pts.append(pt.astype(BF16))
        for g in range(kvh):
            acc_scr[g] = acc_scr[g] + _dot(vt_of(g), pts[g])

    chunk(kc_ref[...], lambda g: vtc_ref[0, g * hd:(g + 1) * hd, :])
    if n_lat_chunks:
        def body(j, carry):
            off = pl.multiple_of(j * tk, tk)
            chunk(kl_ref[pl.ds(off, tk), :], lambda g: vtl_ref[j, g * hd:(g + 1) * hd, :])
            return carry

        lax.fori_loop(0, n_lat_chunks, body, 0)

    for g in range(kvh):
        out = acc_scr[g] * (1.0 / l_scr[g])
        for u in range(grp):
            h = g * grp + u
            o_ref[h * hd:(h + 1) * hd, :] = out[:, u * tq:(u + 1) * tq].astype(BF16)


def flash_attention(dims, qt, k, vt_ctx, vt_lat, hd, q_tile0, n_q_tiles, n_lat_chunks):
    aw, kw = qt.shape[0], k.shape[1]
    tq = tk = dims.tr
    kvh = kw // hd
    grp = aw // kw
    ctx0 = dims.n_lat // dims.c
    in_specs = [pl.BlockSpec((aw, tq), lambda bi, qi: (0, q_tile0 + bi * n_q_tiles + qi)),
                pl.BlockSpec((dims.c, kw), lambda bi, qi: (ctx0 + bi, 0)),
                pl.BlockSpec((1, kw, dims.c), lambda bi, qi: (bi, 0, 0))]
    args = [qt, k, vt_ctx]
    if n_lat_chunks:
        in_specs += [pl.BlockSpec((dims.s, kw), lambda bi, qi: (bi, 0)),
                     pl.BlockSpec((n_lat_chunks, kw, tk), lambda bi, qi: (bi, 0, 0))]
        args += [k, vt_lat]
    return pl.pallas_call(
        functools.partial(_flash_kernel, kvh=kvh, grp=grp, hd=hd, tq=tq, tk=tk, n_lat_chunks=n_lat_chunks),
        grid=(dims.b, n_q_tiles),
        in_specs=in_specs,
        out_specs=pl.BlockSpec((aw, tq), lambda bi, qi: (0, bi * n_q_tiles + qi)),
        out_shape=jax.ShapeDtypeStruct((aw, dims.b * n_q_tiles * tq), BF16),
        scratch_shapes=[pltpu.VMEM((kvh, kw, grp * tq), BF16), pltpu.VMEM((kvh, 1, grp * tq), F32),
                        pltpu.VMEM((kvh, 1, grp * tq), F32), pltpu.VMEM((kvh, hd, grp * tq), F32)],
        compiler_params=_params(("parallel", "parallel")),
        name="flash_attention",
    )(*args)


def attention(dims, qt, k, v, hd, ctx_out):
    b, s, c, tr = dims.b, dims.s, dims.c, dims.tr
    kw = k.shape[1]
    nl = dims.n_lat
    vt_ctx = v[nl:].reshape(b, c, kw).transpose(0, 2, 1)
    vt_lat = v[:nl].reshape(b * (s // tr), tr, kw).transpose(0, 2, 1)
    att = flash_attention(dims, qt, k, vt_ctx, vt_lat, hd, 0, s // tr, s // tr)
    if not ctx_out:
        return att
    att_c = flash_attention(dims, qt, k, vt_ctx, None, hd, nl // tr, c // tr, 0)
    return jnp.concatenate([att, att_c], axis=1)


def _lru_prep_kernel(x_ref, xh_ref, cw_ref, cb_ref, wrh_ref, wrl_ref, brg_ref, wih_ref, wil_ref, big_ref,
                     lam_ref, a_ref, b_ref, x_scr, *, dims, reverse, taps):
    first, last = _seg_edges(dims, pl.program_id(0))
    tr = dims.tr
    if reverse:
        x_scr[0:tr, :] = x_ref[...]
        x_scr[tr:, :] = jnp.where(last, 0.0, xh_ref[...])
        base = 0
    else:
        x_scr[0:8, :] = jnp.where(first, 0.0, xh_ref[...])
        x_scr[8:, :] = x_ref[...]
        base = 8 - (taps - 1)
    xc = jnp.zeros((tr, x_scr.shape[1]), F32) + cb_ref[...]
    for j in range(taps):
        xc = xc + cw_ref[j:j + 1, :] * x_scr[pl.ds(base + j, tr), :]
    cwd = wrh_ref.shape[1]
    rg, ig = [], []
    for c in range(wrh_ref.shape[0]):
        xcc = xc[:, c * cwd:(c + 1) * cwd]
        rg.append(_dot3(xcc, wrh_ref[c], wrl_ref[c]))
        ig.append(_dot3(xcc, wih_ref[c], wil_ref[c]))
    cat = lambda xs: jnp.concatenate(xs, axis=1) if len(xs) > 1 else xs[0]
    rg = _sigmoid(cat(rg) + brg_ref[...])
    ig = _sigmoid(cat(ig) + big_ref[...])
    nl = -lam_ref[...]
    softplus = jnp.maximum(nl, 0.0) + jnp.log1p(jnp.exp(-jnp.abs(nl)))
    log_a = -LRU_C * rg * softplus
    a = jnp.exp(log_a)
    a_ref[...] = a
    b_ref[...] = jnp.sqrt(1.0 - a * a) * (ig * xc)


def _block_diag(wb, cwd):
    nb, bw, _ = wb.shape
    per = cwd // bw
    wb = wb.reshape(nb // per, per, bw, bw)
    eye = jnp.eye(per, dtype=wb.dtype)
    return jnp.einsum("cpij,pq->cpiqj", wb, eye).reshape(nb // per, cwd, cwd)


def lru_prep(dims, z_main, col, p, d_idx):
    reverse = d_idx == 1
    tr, n = dims.tr, dims.n
    taps, lw = p["conv_w"].shape[-2:]
    cwd = min(MXU_DIM, lw)
    wrh, wrl = _split(_block_diag(p["w_rg"][d_idx], cwd))
    wih, wil = _split(_block_diag(p["w_ig"][d_idx], cwd))
    per = tr // 8
    nb = n // 8
    if reverse:
        halo = pl.BlockSpec((8, lw), lambda i: (jnp.minimum((i + 1) * per, nb - 1), col))
    else:
        halo = pl.BlockSpec((8, lw), lambda i: (jnp.maximum(i * per - 1, 0), col))
    vec = pl.BlockSpec((1, lw), lambda i: (0, 0))
    mat = pl.BlockSpec((lw // cwd, cwd, cwd), lambda i: (0, 0, 0))
    return pl.pallas_call(
        functools.partial(_lru_prep_kernel, dims=dims, reverse=reverse, taps=taps),
        grid=(n // tr,),
        in_specs=[pl.BlockSpec((tr, lw), lambda i: (i, col)), halo,
                  pl.BlockSpec((taps, lw), lambda i: (0, 0)), vec, mat, mat, vec, mat, mat, vec, vec],
        out_specs=[pl.BlockSpec((tr, lw), lambda i: (i, 0))] * 2,
        out_shape=[jax.ShapeDtypeStruct((n, lw), F32)] * 2,
        scratch_shapes=[pltpu.VMEM((tr + 8, lw), F32)],
        compiler_params=_params(("parallel",)),
        name="lru_prep",
    )(z_main, z_main, p["conv_w"][d_idx], p["conv_b"][d_idx].reshape(1, lw), wrh, wrl,
      p["b_rg"][d_idx].reshape(1, lw), wih, wil, p["b_ig"][d_idx].reshape(1, lw),
      p["lam"][d_idx].reshape(1, lw))


SUBLANES = 8


def _lru_scan_kernel(a_ref, b_ref, h_ref, h_scr, *, tr, reverse):
    @pl.when(pl.program_id(1) == 0)
    def _():
        h_scr[...] = jnp.zeros_like(h_scr)

    n_groups = tr // SUBLANES
    rid = lax.broadcasted_iota(jnp.int32, (SUBLANES, 1), 0)

    def group(gi, h):
        g = n_groups - 1 - gi if reverse else gi
        r0 = pl.multiple_of(g * SUBLANES, SUBLANES)
        a = a_ref[pl.ds(r0, SUBLANES), :]
        b = b_ref[pl.ds(r0, SUBLANES), :]
        for sh in (1, 2, 4):
            if reverse:
                has = rid < SUBLANES - sh
                a_sh = jnp.where(has, pltpu.roll(a, SUBLANES - sh, axis=0), 1.0)
                b_sh = jnp.where(has, pltpu.roll(b, SUBLANES - sh, axis=0), 0.0)
            else:
                has = rid >= sh
                a_sh = jnp.where(has, pltpu.roll(a, sh, axis=0), 1.0)
                b_sh = jnp.where(has, pltpu.roll(b, sh, axis=0), 0.0)
            b = a * b_sh + b
            a = a * a_sh
        hs = a * h + b
        h_ref[pl.ds(r0, SUBLANES), :] = hs
        return hs[0:1] if reverse else hs[SUBLANES - 1:]

    h_scr[...] = lax.fori_loop(0, n_groups, group, h_scr[...])


def lru_scan(dims, a, b, reverse):
    tr, s, bsz = dims.tr, dims.s, dims.b
    lw = a.shape[-1]
    lat_tiles = s // tr
    ctx0 = dims.n_lat // tr

    def tmap(bi, j):
        lat = lat_tiles - j if reverse else j - 1
        return (jnp.where(j == 0, ctx0 + bi, bi * lat_tiles + lat), 0)

    spec = pl.BlockSpec((tr, lw), tmap)
    return pl.pallas_call(
        functools.partial(_lru_scan_kernel, tr=tr, reverse=reverse),
        grid=(bsz, lat_tiles + 1),
        in_specs=[spec, spec],
        out_specs=spec,
        out_shape=jax.ShapeDtypeStruct(a.shape, F32),
        scratch_shapes=[pltpu.VMEM((1, lw), F32)],
        compiler_params=_params(("arbitrary", "arbitrary")),
        name="lru_scan",
    )(a, b)


def _lru_gate_kernel(h0_ref, h1_ref, g_ref, o_ref):
    g = g_ref[...]
    gelu = 0.5 * g * (1.0 + jnp.tanh(math.sqrt(2.0 / math.pi) * (g + 0.044715 * (g * g * g))))
    o_ref[...] = ((h0_ref[...] + h1_ref[...]) * gelu).astype(BF16)


def lru_gate(dims, h0, h1, z_main, col):
    tr, n = dims.tr, dims.n
    lw = h0.shape[-1]
    row = pl.BlockSpec((tr, lw), lambda i: (i, 0))
    return pl.pallas_call(
        _lru_gate_kernel,
        grid=(n // tr,),
        in_specs=[row, row, pl.BlockSpec((tr, lw), lambda i: (i, col))],
        out_specs=row,
        out_shape=jax.ShapeDtypeStruct((n, lw), BF16),
        compiler_params=_params(("parallel",)),
        name="lru_gate",
    )(h0, h1, z_main)


def odd_mixer(dims, x, g_norm, mod, p, hd, ctx_out, n_out_rows):
    n = dims.n
    lw = p["lam"].shape[-1]
    w_in = p["w_in"]
    aw = p["w_out"].shape[0] - lw
    kw = (w_in.shape[1] - aw - 2 * lw) // 2
    assert aw == lw
    main_cols = jnp.concatenate([w_in[:, :aw], w_in[:, aw + 2 * kw:]], axis=1)
    z_main = nm_matmul(dims, x, g_norm, mod["sh1"], mod["sc1"], main_cols.astype(BF16), n)
    z_kv = nm_matmul(dims, x, g_norm, mod["sh1"], mod["sc1"], w_in[:, aw:aw + 2 * kw].astype(BF16), n)
    q, k, v = qkv_prep(dims, z_main, z_kv, p["qn_g"], p["kn_g"], aw, kw, hd)
    att = attention(dims, q, k, v, hd, ctx_out)
    ab = [lru_prep(dims, z_main, 1, p, d_idx) for d_idx in range(2)]
    hs = [lru_scan(dims, ab[d_idx][0], ab[d_idx][1], d_idx == 1) for d_idx in range(2)]
    hg = lru_gate(dims, hs[0], hs[1], z_main, 2)
    w_out = p["w_out"].astype(BF16)
    return matmul_residual(dims, att, hg, w_out[:aw], w_out[aw:], x, mod["g1"], n_out_rows, a1_feature_major=True)


def _moe_kernel(be_ref, na_ref, x_ref, wg_ref, wu_ref, wd_ref, gt_ref, o_ref):
    i = pl.program_id(0)

    @pl.when(i < na_ref[0])
    def _():
        x = x_ref[...]
        hg = _dot(x, wg_ref[0])
        hu = _dot(x, wu_ref[0])
        h = (hg * _sigmoid(hg) * hu).astype(BF16)
        o_ref[...] = _dot(h, wd_ref[0]) * gt_ref[...]

    @pl.when(i >= na_ref[0])
    def _():
        o_ref[...] = jnp.zeros_like(o_ref)


def moe_experts(xs, gate_rows, blk_expert, n_active, w_gate, w_up, w_down):
    rows, d = xs.shape
    n_blk = rows // MOE_ROWS
    de = w_gate.shape[-1]
    grid_spec = pltpu.PrefetchScalarGridSpec(
        num_scalar_prefetch=2,
        grid=(n_blk,),
        in_specs=[pl.BlockSpec((MOE_ROWS, d), lambda i, be, na: (i, 0)),
                  pl.BlockSpec((1, d, de), lambda i, be, na: (be[i], 0, 0)),
                  pl.BlockSpec((1, d, de), lambda i, be, na: (be[i], 0, 0)),
                  pl.BlockSpec((1, de, d), lambda i, be, na: (be[i], 0, 0)),
                  pl.BlockSpec((MOE_ROWS, 1), lambda i, be, na: (i, 0))],
        out_specs=pl.BlockSpec((MOE_ROWS, d), lambda i, be, na: (i, 0)),
    )
    return pl.pallas_call(
        _moe_kernel,
        grid_spec=grid_spec,
        out_shape=jax.ShapeDtypeStruct((rows, d), F32),
        compiler_params=_params(("arbitrary",)),
        name="moe_experts",
    )(blk_expert, n_active, xs, w_gate, w_up, w_down, gate_rows)


def hierarchical_moe(dims, x, g_norm, mod, p, n_rows):
    n_grp = p["w_grp"].shape[1]
    epg = p["w_router"].shape[-1]
    n_exp = n_grp * epg
    d = dims.d
    w_rt = jnp.concatenate([p["w_grp"], p["w_router"].transpose(1, 0, 2).reshape(d, n_exp)], axis=1)
    b_rt = jnp.concatenate([p["b_grp"], p["b_router"].reshape(n_exp)])
    padc = -w_rt.shape[1] % LANES
    f, rt = nm_router(dims, x, g_norm, mod["sh2"], mod["sc2"], jnp.pad(w_rt, ((0, 0), (0, padc))),
                      jnp.pad(b_rt, (0, padc)), n_rows, n_grp, epg)

    gates = rt[:, :TOP_K]
    flat_e = rt[:, TOP_K:2 * TOP_K].astype(jnp.int32).reshape(-1)
    n_assign = n_rows * TOP_K
    i32 = jnp.int32
    order = jnp.argsort(flat_e).astype(i32)
    inv = jnp.argsort(order).astype(i32)
    e_sorted = flat_e[order]
    counts = jnp.sum(flat_e[:, None] == jnp.arange(n_exp, dtype=i32)[None, :], axis=0, dtype=i32)
    padded = (counts + MOE_ROWS - 1) // MOE_ROWS * MOE_ROWS
    seg_start = jnp.cumsum(counts) - counts
    pad_end = jnp.cumsum(padded)
    pad_start = pad_end - padded
    slot = pad_start[e_sorted] + jnp.arange(n_assign, dtype=i32) - seg_start[e_sorted]
    n_blk = -(-n_assign // MOE_ROWS) + n_exp
    blk_expert = jnp.minimum(jnp.sum(jnp.arange(n_blk, dtype=i32)[:, None] >= (pad_end // MOE_ROWS)[None, :],
                                     axis=1, dtype=i32), n_exp - 1)
    n_active = (pad_end[-1] // MOE_ROWS).astype(i32).reshape(1)
    rank = (jnp.arange(n_blk, dtype=i32) * MOE_ROWS - pad_start[blk_expert])[:, None] \
        + jnp.arange(MOE_ROWS, dtype=i32)[None, :]
    valid = (rank < counts[blk_expert][:, None]).reshape(-1)
    pos = jnp.clip(seg_start[blk_expert][:, None] + rank, 0, n_assign - 1).reshape(-1)
    src = jnp.where(valid, order[pos] // TOP_K, 0)
    gate_rows = jnp.where(valid, gates.reshape(-1)[order][pos], 0.0)
    slot_of = slot[inv].reshape(n_rows, TOP_K)

    xs = jnp.take(f, src, axis=0, mode="clip")
    out = moe_experts(xs, gate_rows[:, None], blk_expert, n_active, p["w_gate"].astype(BF16),
                      p["w_up"].astype(BF16), p["w_down"].astype(BF16))
    y0 = jnp.take(out, slot_of[:, 0], axis=0, mode="clip")
    y1 = jnp.take(out, slot_of[:, 1], axis=0, mode="clip")
    return combine_residual(dims, x, y0, y1, mod["g2"], n_rows)


def kernel(x, c, ctx, c_ctx, norm_mix_g, norm_ffn_g, ada_w, ada_b, even_w_in, rwkv_mu, rwkv_w0, rwkv_w_up, rwkv_a0, rwkv_a_up, rwkv_k_k, rwkv_k_a, rwkv_r_k, rwkv_g_up, rwkv_gn_g, rwkv_gn_b, conv_w, conv_b, conv_ln_g, conv_ln_b, even_w_out, odd_w_in, att_qn_g, att_kn_g, lru_conv_w, lru_conv_b, lru_w_rg, lru_b_rg, lru_w_ig, lru_b_ig, lru_lambda, odd_w_out, moe_w_grp, moe_b_grp, moe_w_router, moe_b_router, moe_w_gate, moe_w_up, moe_w_down, final_g):
    bsz, s, d = x.shape
    n_ctx = ctx.shape[1]
    depth = norm_mix_g.shape[0]
    hd = att_qn_g.shape[-1]
    dims = _Dims(bsz, s, n_ctx, d)
    xa = jnp.concatenate([x.reshape(bsz * s, d), ctx.reshape(bsz * n_ctx, d)], axis=0)

    cond = jnp.concatenate([c, c_ctx[None, :]], axis=0)
    cond = cond * _sigmoid(cond)
    rows = -(bsz + 1) % 8
    cond = jnp.pad(cond, ((0, rows), (0, 0)))

    for layer in range(depth):
        ctx_out = layer < depth - 1
        n_rows = dims.n if ctx_out else dims.n_lat
        j = layer // 2
        mod_all = ada_matmul(cond, ada_w[layer], ada_b[layer])[:bsz + 1]
        names = ("sh1", "sc1", "g1", "sh2", "sc2", "g2")
        mod = {nm: mod_all[:, q * d:(q + 1) * d].reshape(bsz + 1, 1, d) for q, nm in enumerate(names)}
        if layer % 2 == 0:
            p = dict(w_in=even_w_in[j], mu=rwkv_mu[j], w0=rwkv_w0[j], w_up=rwkv_w_up[j], a0=rwkv_a0[j],
                     a_up=rwkv_a_up[j], k_k=rwkv_k_k[j], k_a=rwkv_k_a[j], r_k=rwkv_r_k[j].reshape(-1),
                     g_up=rwkv_g_up[j], gn_g=rwkv_gn_g[j], gn_b=rwkv_gn_b[j], conv_w=conv_w[j],
                     conv_b=conv_b[j], ln_g=conv_ln_g[j], ln_b=conv_ln_b[j], w_out=even_w_out[j])
            x_new = even_mixer(dims, xa, norm_mix_g[layer], mod, p, hd, n_rows)
        else:
            p = dict(w_in=odd_w_in[j], qn_g=att_qn_g[j], kn_g=att_kn_g[j], conv_w=lru_conv_w[j],
                     conv_b=lru_conv_b[j], w_rg=lru_w_rg[j], b_rg=lru_b_rg[j].reshape(2, -1),
                     w_ig=lru_w_ig[j], b_ig=lru_b_ig[j].reshape(2, -1), lam=lru_lambda[j], w_out=odd_w_out[j])
            x_new = odd_mixer(dims, xa, norm_mix_g[layer], mod, p, hd, ctx_out, n_rows)
        pm = dict(w_grp=moe_w_grp[layer], b_grp=moe_b_grp[layer], w_router=moe_w_router[layer],
                  b_router=moe_b_router[layer], w_gate=moe_w_gate[layer], w_up=moe_w_up[layer],
                  w_down=moe_w_down[layer])
        xa = hierarchical_moe(dims, x_new, norm_ffn_g[layer], mod, pm, n_rows)
    return final_norm(dims, xa, final_g, dims.n_lat).reshape(bsz, s, d)
```

```python
import functools
import math

import jax
import jax.numpy as jnp
from jax import lax
from jax.experimental import pallas as pl
from jax.experimental.pallas import tpu as pltpu

F32 = jnp.float32
BF16 = jnp.bfloat16

GRID_W = 64
ROPE_THETA = 10000.0
RWKV_GN_EPS = 64e-5
LRU_C = 8.0
TOP_K = 2
RMS_EPS = 1e-6
LN_EPS = 1e-5
LANES = 128
SUBLANES = 8
MXU_DIM = 256
VMEM_LIMIT = 48 * 2**20
MOE_ROWS = 256
SCAN_TB = 32


def _params(sem):
    return pltpu.CompilerParams(dimension_semantics=sem, vmem_limit_bytes=VMEM_LIMIT)


def _dot(a, b):
    return jnp.dot(a, b, preferred_element_type=F32)


def _split(w):
    hi = w.astype(BF16)
    return hi, (w - hi.astype(F32)).astype(BF16)


def _dot3(a, w_hi, w_lo):
    a_hi = a.astype(BF16)
    a_lo = (a - a_hi.astype(F32)).astype(BF16)
    return _dot(a_hi, w_hi) + _dot(a_lo, w_hi) + _dot(a_hi, w_lo)


def _sigmoid(x):
    return 1.0 / (1.0 + jnp.exp(-x))


def _pow2_tile(limit, *dims):
    t = limit
    while any(d % t for d in dims):
        t //= 2
    return t


class _Dims:
    def __init__(self, b, s, c, d):
        self.b, self.s, self.c, self.d = b, s, c, d
        self.n_lat = b * s
        self.n = b * s + b * c
        self.tm = _pow2_tile(1024, s, b * c)
        self.tr = _pow2_tile(256, s, c)
        assert self.tr == c, "sequence kernels assume one context tile per batch element"

    def mod_index(self, tile):
        n_lat_tiles = self.n_lat // tile

        def f(i):
            return jnp.where(i < n_lat_tiles, (i * tile) // self.s, self.b)
        return f

    def seq_index(self, i):
        per_seq = self.s // self.tr
        n_lat_tiles = self.n_lat // self.tr
        return jnp.where(i < n_lat_tiles, (i // per_seq) * (per_seq + 1) + 1 + i % per_seq,
                         (i - n_lat_tiles) * (per_seq + 1))


def _ada_kernel(a_ref, wh_ref, wl_ref, b_ref, o_ref):
    o_ref[...] = _dot3(a_ref[...], wh_ref[...], wl_ref[...]) + b_ref[...]


def ada_matmul(a, w, bias):
    r, d = a.shape
    m = w.shape[1]
    tn = _pow2_tile(1024, m)
    wh, wl = _split(w)
    return pl.pallas_call(
        _ada_kernel,
        grid=(m // tn,),
        in_specs=[pl.BlockSpec((r, d), lambda j: (0, 0)),
                  pl.BlockSpec((d, tn), lambda j: (0, j)),
                  pl.BlockSpec((d, tn), lambda j: (0, j)),
                  pl.BlockSpec((1, tn), lambda j: (0, j))],
        out_specs=pl.BlockSpec((r, tn), lambda j: (0, j)),
        out_shape=jax.ShapeDtypeStruct((r, m), F32),
        compiler_params=_params(("arbitrary",)),
        name="ada_matmul",
    )(a, wh, wl, bias.reshape(1, m))


def _norm_mod(x_ref, g_ref, sh_ref, sc_ref):
    xf = x_ref[...]
    ms = jnp.mean(xf * xf, axis=-1, keepdims=True)
    y = xf * lax.rsqrt(ms + RMS_EPS) * g_ref[...]
    return y * (1.0 + sc_ref[0]) + sh_ref[0]


def _nm_matmul_kernel(x_ref, g_ref, sh_ref, sc_ref, w_ref, z_ref, h_scr):
    @pl.when(pl.program_id(1) == 0)
    def _():
        h_scr[...] = _norm_mod(x_ref, g_ref, sh_ref, sc_ref).astype(BF16)

    z_ref[...] = _dot(h_scr[...], w_ref[...])


def nm_matmul(dims, x, g, shift, scale, w, n_rows):
    d = dims.d
    m = w.shape[1]
    tm = dims.tm
    tn = _pow2_tile(1024, m)
    midx = dims.mod_index(tm)
    return pl.pallas_call(
        _nm_matmul_kernel,
        grid=(n_rows // tm, m // tn),
        in_specs=[pl.BlockSpec((tm, d), lambda i, j: (i, 0)),
                  pl.BlockSpec((1, d), lambda i, j: (0, 0)),
                  pl.BlockSpec((1, 1, d), lambda i, j: (midx(i), 0, 0)),
                  pl.BlockSpec((1, 1, d), lambda i, j: (midx(i), 0, 0)),
                  pl.BlockSpec((d, tn), lambda i, j: (0, j))],
        out_specs=pl.BlockSpec((tm, tn), lambda i, j: (i, j)),
        out_shape=jax.ShapeDtypeStruct((n_rows, m), F32),
        scratch_shapes=[pltpu.VMEM((tm, d), BF16)],
        compiler_params=_params(("parallel", "arbitrary")),
        name="nm_matmul",
    )(x, g.reshape(1, d), shift, scale, w)


def _nm_router_kernel(x_ref, g_ref, sh_ref, sc_ref, wh_ref, wl_ref, b_ref, f_ref, rt_ref, *, n_grp, epg):
    h = _norm_mod(x_ref, g_ref, sh_ref, sc_ref)
    f_ref[...] = h.astype(BF16)
    lg = _dot3(h, wh_ref[...], wl_ref[...]) + b_ref[...]
    lane = lax.broadcasted_iota(jnp.int32, (1, lg.shape[1]), 1).astype(F32)
    neg = -jnp.inf
    big = float(lg.shape[1])

    def first_max(vals):
        mx = jnp.max(vals, axis=-1, keepdims=True)
        return mx, jnp.min(jnp.where(vals == mx, lane, big), axis=-1, keepdims=True)

    gl = jnp.where(lane < n_grp, lg, neg)
    gmax, gidx = first_max(gl)
    grp_p = 1.0 / jnp.sum(jnp.exp(gl - gmax), axis=-1, keepdims=True)
    lo = n_grp + gidx * epg
    el = jnp.where(lane >= lo, jnp.where(lane < lo + epg, lg, neg), neg)
    m1, i1 = first_max(el)
    m2, i2 = first_max(jnp.where(lane == i1, neg, el))
    p2 = jnp.exp(m2 - m1)
    g1 = grp_p / (1.0 + p2)
    g2 = grp_p * p2 / (1.0 + p2)
    rt_ref[...] = jnp.where(lane == 0, g1, jnp.where(lane == 1, g2, jnp.where(
        lane == 2, i1 - n_grp, jnp.where(lane == 3, i2 - n_grp, 0.0))))


def nm_router(dims, x, g, shift, scale, w_rt, b_rt, n_rows, n_grp, epg):
    d = dims.d
    tm = dims.tm
    m = w_rt.shape[1]
    wh, wl = _split(w_rt)
    midx = dims.mod_index(tm)
    return pl.pallas_call(
        functools.partial(_nm_router_kernel, n_grp=n_grp, epg=epg),
        grid=(n_rows // tm,),
        in_specs=[pl.BlockSpec((tm, d), lambda i: (i, 0)),
                  pl.BlockSpec((1, d), lambda i: (0, 0)),
                  pl.BlockSpec((1, 1, d), lambda i: (midx(i), 0, 0)),
                  pl.BlockSpec((1, 1, d), lambda i: (midx(i), 0, 0)),
                  pl.BlockSpec((d, m), lambda i: (0, 0)),
                  pl.BlockSpec((d, m), lambda i: (0, 0)),
                  pl.BlockSpec((1, m), lambda i: (0, 0))],
        out_specs=[pl.BlockSpec((tm, d), lambda i: (i, 0)),
                   pl.BlockSpec((tm, m), lambda i: (i, 0))],
        out_shape=[jax.ShapeDtypeStruct((n_rows, d), BF16),
                   jax.ShapeDtypeStruct((n_rows, m), F32)],
        compiler_params=_params(("parallel",)),
        name="nm_router",
    )(x, g.reshape(1, d), shift, scale, wh, wl, b_rt.reshape(1, m))


def _mm_res_kernel(a1_ref, a2_ref, w1_ref, w2_ref, x_ref, gt_ref, o_ref, *, a1_feature_major):
    if a1_feature_major:
        acc = lax.dot_general(a1_ref[...], w1_ref[...], (((0,), (0,)), ((), ())), preferred_element_type=F32)
    else:
        acc = _dot(a1_ref[...], w1_ref[...])
    acc = acc + _dot(a2_ref[...], w2_ref[...])
    o_ref[...] = x_ref[...] + gt_ref[0] * acc


def matmul_residual(dims, a1, a2, w1, w2, x, gate, n_rows, a1_feature_major=False):
    d = dims.d
    tm = dims.tm
    tn = _pow2_tile(512, d)
    k1, k2 = w1.shape[0], a2.shape[1]
    midx = dims.mod_index(tm)
    a1_spec = pl.BlockSpec((k1, tm), lambda i, j: (0, i)) if a1_feature_major else \
        pl.BlockSpec((tm, k1), lambda i, j: (i, 0))
    out = pl.pallas_call(
        functools.partial(_mm_res_kernel, a1_feature_major=a1_feature_major),
        grid=(n_rows // tm, d // tn),
        in_specs=[a1_spec,
                  pl.BlockSpec((tm, k2), lambda i, j: (i, 0)),
                  pl.BlockSpec((k1, tn), lambda i, j: (0, j)),
                  pl.BlockSpec((k2, tn), lambda i, j: (0, j)),
                  pl.BlockSpec((tm, tn), lambda i, j: (i, j)),
                  pl.BlockSpec((1, 1, tn), lambda i, j: (midx(i), 0, j))],
        out_specs=pl.BlockSpec((tm, tn), lambda i, j: (i, j)),
        out_shape=jax.ShapeDtypeStruct((n_rows, d), F32),
        compiler_params=_params(("parallel", "arbitrary")),
        name="matmul_residual",
    )(a1, a2, w1, w2, x, gate)
    return out


def _combine_kernel(x_ref, y0_ref, y1_ref, gt_ref, o_ref):
    o_ref[...] = x_ref[...] + gt_ref[0] * (y0_ref[...].astype(F32) + y1_ref[...].astype(F32))


def combine_residual(dims, x, y0, y1, gate, n_rows):
    d = dims.d
    tm = _pow2_tile(512, dims.tm)
    midx = dims.mod_index(tm)
    row = pl.BlockSpec((tm, d), lambda i: (i, 0))
    return pl.pallas_call(
        _combine_kernel,
        grid=(n_rows // tm,),
        in_specs=[row, row, row, pl.BlockSpec((1, 1, d), lambda i: (midx(i), 0, 0))],
        out_specs=row,
        out_shape=jax.ShapeDtypeStruct((n_rows, d), F32),
        compiler_params=_params(("parallel",)),
        name="combine_residual",
    )(x, y0, y1, gate)


def _final_norm_kernel(x_ref, g_ref, o_ref):
    xf = x_ref[...]
    ms = jnp.mean(xf * xf, axis=-1, keepdims=True)
    o_ref[...] = xf * lax.rsqrt(ms + RMS_EPS) * g_ref[...]


def final_norm(dims, x, g, n_rows):
    d = dims.d
    tm = _pow2_tile(512, dims.tm)
    return pl.pallas_call(
        _final_norm_kernel,
        grid=(n_rows // tm,),
        in_specs=[pl.BlockSpec((tm, d), lambda i: (i, 0)), pl.BlockSpec((1, d), lambda i: (0, 0))],
        out_specs=pl.BlockSpec((tm, d), lambda i: (i, 0)),
        out_shape=jax.ShapeDtypeStruct((n_rows, d), F32),
        compiler_params=_params(("parallel",)),
        name="final_norm",
    )(x, g.reshape(1, d))


def _seg_edges(dims, i):
    tr = dims.tr
    n_lat_tiles = dims.n_lat // tr
    per_seq = dims.s // tr
    is_lat = i < n_lat_tiles
    first = jnp.where(is_lat, i % per_seq == 0, True)
    last = jnp.where(is_lat, i % per_seq == per_seq - 1, True)
    return first, last


def _block_ones(hd):
    idx = jnp.arange(LANES) // hd
    return (idx[:, None] == idx[None, :]).astype(BF16)


def _seg_sum(x, q):
    outs = []
    for c in range(x.shape[1] // LANES):
        xc = x[:, c * LANES:(c + 1) * LANES]
        hi = xc.astype(BF16)
        lo = (xc - hi.astype(F32)).astype(BF16)
        outs.append(_dot(hi, q) + _dot(lo, q))
    return jnp.concatenate(outs, axis=1) if len(outs) > 1 else outs[0]


def _neighbour(x, halo_ref, reverse, at_edge):
    rows = x.shape[0]
    rid = lax.broadcasted_iota(jnp.int32, (rows, 1), 0)
    if reverse:
        edge_row = jnp.where(at_edge, 0.0, halo_ref[0:1, :])
        return jnp.where(rid == rows - 1, edge_row, pltpu.roll(x, rows - 1, axis=0))
    edge_row = jnp.where(at_edge, 0.0, halo_ref[halo_ref.shape[0] - 1:, :])
    return jnp.where(rid == 0, edge_row, pltpu.roll(x, 1, axis=0))


def _rwkv_prep_kernel(r_ref, k_ref, v_ref, lw_ref, la_ref, rh_ref, kh_ref, vh_ref, lwh_ref, lah_ref,
                      mur_ref, muk_ref, muv_ref, mulw_ref, mula_ref, w0_ref, wuh_ref, wul_ref,
                      a0_ref, auh_ref, aul_ref, ka_ref, rk_ref, q_ref,
                      wo_ref, ao_ref, bo_ref, *, dims, reverse):
    first, last = _seg_edges(dims, pl.program_id(0))
    at_edge = last if reverse else first

    def mix(x_ref, h_ref, mu_ref):
        x = x_ref[...]
        return x + (_neighbour(x, h_ref, reverse, at_edge) - x) * mu_ref[...]

    r = mix(r_ref, rh_ref, mur_ref)
    k = mix(k_ref, kh_ref, muk_ref)
    v = mix(v_ref, vh_ref, muv_ref)
    lw = mix(lw_ref, lwh_ref, mulw_ref)
    la = mix(la_ref, lah_ref, mula_ref)
    q = q_ref[...]
    u = w0_ref[...] + _dot3(jnp.tanh(lw), wuh_ref[...], wul_ref[...])
    decay = jnp.exp(-math.exp(-0.5) * _sigmoid(u))
    a = _sigmoid(a0_ref[...] + _dot3(la, auh_ref[...], aul_ref[...]))
    kf = k * (1.0 + (a - 1.0) * ka_ref[...])
    wo_ref[...] = decay
    ao_ref[...] = a
    bo_ref[...] = _seg_sum(r * kf * rk_ref[...], q) * v


def rwkv_prep(dims, z_rkv, z_lora, p, d_idx, hd):
    reverse = d_idx == 1
    tr, n = dims.tr, dims.n
    w = p["w0"].shape[-1]
    lw_w = p["w_up"].shape[-2]
    la_w = p["a_up"].shape[-2]
    assert 2 * lw_w == LANES and 2 * la_w == LANES
    mu = p["mu"][d_idx]

    def half(vec, width):
        z = jnp.zeros_like(vec)
        return jnp.concatenate([vec, z] if d_idx == 0 else [z, vec], axis=0)

    mu_lw = half(mu[3 * w:3 * w + lw_w], lw_w).reshape(1, LANES)
    mu_la = half(mu[3 * w + lw_w:], la_w).reshape(1, LANES)
    wuh, wul = _split(half(p["w_up"][d_idx], lw_w))
    auh, aul = _split(half(p["a_up"][d_idx], la_w))
    row = lambda c: pl.BlockSpec((tr, w), lambda i: (i, c))
    chunk = lambda c: pl.BlockSpec((tr, LANES), lambda i: (i, c))

    def halo_c(c):
        per = tr // 8
        nb = n // 8
        if reverse:
            return pl.BlockSpec((8, LANES), lambda i: (jnp.minimum((i + 1) * per, nb - 1), c))
        return pl.BlockSpec((8, LANES), lambda i: (jnp.maximum(i * per - 1, 0), c))

    def halo_w(c):
        per = tr // 8
        nb = n // 8
        if reverse:
            return pl.BlockSpec((8, w), lambda i: (jnp.minimum((i + 1) * per, nb - 1), c))
        return pl.BlockSpec((8, w), lambda i: (jnp.maximum(i * per - 1, 0), c))

    vec = lambda width: pl.BlockSpec((1, width), lambda i: (0, 0))
    mat = lambda: pl.BlockSpec((LANES, w), lambda i: (0, 0))
    outs = pl.pallas_call(
        functools.partial(_rwkv_prep_kernel, dims=dims, reverse=reverse),
        grid=(n // tr,),
        in_specs=[row(0), row(1), row(2), chunk(0), chunk(1),
                  halo_w(0), halo_w(1), halo_w(2), halo_c(0), halo_c(1),
                  vec(w), vec(w), vec(w), vec(LANES), vec(LANES), vec(w), mat(), mat(),
                  vec(w), mat(), mat(), vec(w), vec(w),
                  pl.BlockSpec((LANES, LANES), lambda i: (0, 0))],
        out_specs=[pl.BlockSpec((tr, w), lambda i: (dims.seq_index(i), 0))] * 2
        + [pl.BlockSpec((tr, w), lambda i: (i, 0))],
        out_shape=[jax.ShapeDtypeStruct((n, w), F32)] * 3,
        compiler_params=_params(("parallel",)),
        name="rwkv_prep",
    )(z_rkv, z_rkv, z_rkv, z_lora, z_lora, z_rkv, z_rkv, z_rkv, z_lora, z_lora,
      mu[:w].reshape(1, w), mu[w:2 * w].reshape(1, w), mu[2 * w:3 * w].reshape(1, w), mu_lw, mu_la,
      p["w0"][d_idx].reshape(1, w), wuh, wul, p["a0"][d_idx].reshape(1, w), auh, aul,
      p["k_a"].reshape(1, w), p["r_k"].reshape(1, w), _block_ones(hd))
    return outs


def _rwkv_scan_kernel(rc_ref, kc_ref, vc_ref, rl_ref, kl_ref, vl_ref, w_ref, a_ref, mur_ref, muk_ref, muv_ref,
                      kkt_ref, kat_ref, y_ref, s_ref, prev_ref, kk_ref, ka_ref, kf_ref, wr_ref, p_ref,
                      *, tb, hd, reverse, n_ctx_blocks):
    i = pl.program_id(0)
    in_ctx = i < n_ctx_blocks

    @pl.when(i == 0)
    def _():
        s_ref[...] = jnp.zeros_like(s_ref)

    @pl.when((i == 0) | (i == n_ctx_blocks))
    def _():
        prev_ref[...] = jnp.zeros_like(prev_ref)

    def step(s, carry):
        t = tb - 1 - s if reverse else s
        r_raw, k_raw, v_raw = lax.cond(in_ctx, lambda: (rc_ref[t], kc_ref[t], vc_ref[t]),
                                       lambda: (rl_ref[t], kl_ref[t], vl_ref[t]))
        r_t = r_raw + (prev_ref[0] - r_raw) * mur_ref[...]
        k_t = k_raw + (prev_ref[1] - k_raw) * muk_ref[...]
        v_t = v_raw + (prev_ref[2] - v_raw) * muv_ref[...]
        prev_ref[0] = r_raw
        prev_ref[1] = k_raw
        prev_ref[2] = v_raw
        a_t = a_ref[t]
        kk = k_t * kkt_ref[...]
        kk = kk * lax.rsqrt(jnp.maximum(jnp.sum(kk * kk, axis=0, keepdims=True), 1e-24))
        ka = kk * a_t
        kf = k_t * (1.0 + (a_t - 1.0) * kat_ref[...])
        p_prev = p_ref[...]
        p_new = p_prev * w_ref[t]
        p_ref[...] = p_new
        p_inv = 1.0 / p_new
        kk_ref[...] = kk * p_prev
        ka_ref[...] = ka * p_inv
        kf_ref[...] = kf * p_inv
        wr_ref[...] = r_t * p_new
        c1 = jnp.sum(ka * r_t, axis=0, keepdims=True)
        c2 = jnp.sum(kf * r_t, axis=0, keepdims=True)
        acc_sa = jnp.zeros_like(v_t)
        acc_u = jnp.zeros_like(v_t)
        for kx in range(hd):
            sk = s_ref[kx]
            acc_sa = acc_sa + sk * kk_ref[pl.ds(kx, 1), :]
            acc_u = acc_u + sk * wr_ref[pl.ds(kx, 1), :]
        sa = -acc_sa
        y_ref[t] = acc_u + sa * c1 + v_t * c2
        for kx in range(hd):
            s_ref[kx] = s_ref[kx] + sa * ka_ref[pl.ds(kx, 1), :] + v_t * kf_ref[pl.ds(kx, 1), :]
        return carry

    p_ref[...] = jnp.ones_like(p_ref)
    lax.fori_loop(0, tb, step, 0)
    for kx in range(hd):
        s_ref[kx] = s_ref[kx] * p_ref[pl.ds(kx, 1), :]


def rwkv_scan(rkv_ctx, rkv_lat, w, a, tabs, reverse):
    t_all, hd, lanes = w.shape
    n_ctx = rkv_ctx[0].shape[0]
    tb = _pow2_tile(SCAN_TB, n_ctx, t_all - n_ctx)
    n_blocks = t_all // tb
    n_ctx_blocks = n_ctx // tb
    n_lat_blocks = n_blocks - n_ctx_blocks

    def ctx_blk(i):
        j = jnp.minimum(i, n_ctx_blocks - 1)
        return n_ctx_blocks - 1 - j if reverse else j

    def lat_blk(i):
        j = jnp.maximum(i - n_ctx_blocks, 0)
        return n_lat_blocks - 1 - j if reverse else j

    def seq_blk(i):
        return jnp.where(i < n_ctx_blocks, ctx_blk(i), n_ctx_blocks + lat_blk(i))

    blk = lambda f: pl.BlockSpec((tb, hd, lanes), lambda i: (f(i), 0, 0))
    tab = pl.BlockSpec((hd, lanes), lambda i: (0, 0))
    tile = pltpu.VMEM((hd, lanes), F32)
    return pl.pallas_call(
        functools.partial(_rwkv_scan_kernel, tb=tb, hd=hd, reverse=reverse, n_ctx_blocks=n_ctx_blocks),
        grid=(n_blocks,),
        in_specs=[blk(ctx_blk)] * 3 + [blk(lat_blk)] * 3 + [blk(seq_blk)] * 2 + [tab] * 5,
        out_specs=blk(seq_blk),
        out_shape=jax.ShapeDtypeStruct(w.shape, F32),
        scratch_shapes=[pltpu.VMEM((hd, hd, lanes), F32), pltpu.VMEM((3, hd, lanes), F32)] + [tile] * 5,
        compiler_params=_params(("arbitrary",)),
        name="rwkv_scan",
    )(*rkv_ctx, *rkv_lat, w, a, *tabs)


def _rwkv_gn_kernel(y0_ref, y1_ref, g_ref, b_ref, o_ref):
    y = y0_ref[...] + y1_ref[...]
    mean = jnp.mean(y, axis=1, keepdims=True)
    yc = y - mean
    var = jnp.mean(yc * yc, axis=1, keepdims=True)
    o_ref[...] = yc * lax.rsqrt(var + RWKV_GN_EPS) * g_ref[...] + b_ref[...]


def rwkv_group_norm(y0, y1, g_tab, b_tab):
    t_all, hd, lanes = y0.shape
    tb = _pow2_tile(64, t_all)
    blk = pl.BlockSpec((tb, hd, lanes), lambda i: (i, 0, 0))
    tab = pl.BlockSpec((1, hd, lanes), lambda i: (0, 0, 0))
    return pl.pallas_call(
        _rwkv_gn_kernel,
        grid=(t_all // tb,),
        in_specs=[blk, blk, tab, tab],
        out_specs=blk,
        out_shape=jax.ShapeDtypeStruct((t_all, hd, lanes), F32),
        compiler_params=_params(("parallel",)),
        name="rwkv_group_norm",
    )(y0, y1, g_tab, b_tab)


def _even_merge_kernel(yn_ref, b0_ref, b1_ref, zg_ref, val_ref, gate_ref, valp_ref, gatep_ref, valn_ref,
                       gaten_ref, guh_ref, gul_ref, cw_ref, cb_ref, lng_ref, lnb_ref,
                       a_ref, bo_ref, u_scr, us_scr, *, dims, taps, halo):
    first, last = _seg_edges(dims, pl.program_id(0))
    tr = dims.tr
    g = _dot3(_sigmoid(zg_ref[...]), guh_ref[...], gul_ref[...])
    a_ref[...] = ((yn_ref[...] + b0_ref[...] + b1_ref[...]) * g).astype(BF16)

    glu = lambda v_ref, g_ref: v_ref[...] * _sigmoid(g_ref[...])
    u_scr[0:halo, :] = jnp.where(first, 0.0, glu(valp_ref, gatep_ref))
    u_scr[halo:halo + tr, :] = glu(val_ref, gate_ref)
    u_scr[halo + tr:, :] = jnp.where(last, 0.0, glu(valn_ref, gaten_ref))
    pad = taps // 2
    rows = us_scr.shape[1]
    for s in range(1, SUBLANES):
        us_scr[s - 1] = u_scr[pl.ds(s, rows), :]
    acc = jnp.zeros((tr, u_scr.shape[1]), F32) + cb_ref[...]
    for j in range(taps):
        q, s = divmod(halo - pad + j, SUBLANES)
        win = u_scr[pl.ds(q * SUBLANES, tr), :] if s == 0 else us_scr[s - 1, pl.ds(q * SUBLANES, tr), :]
        acc = acc + cw_ref[j:j + 1, :] * win
    mu = jnp.mean(acc, axis=-1, keepdims=True)
    xc = acc - mu
    var = jnp.mean(xc * xc, axis=-1, keepdims=True)
    y = xc * lax.rsqrt(var + LN_EPS) * lng_ref[...] + lnb_ref[...]
    bo_ref[...] = (y * _sigmoid(y)).astype(BF16)


def even_merge(dims, yn, bonus0, bonus1, z_lora, z_cv, p):
    tr, n = dims.tr, dims.n
    w = yn.shape[1]
    taps, cw = p["conv_w"].shape
    lg = p["g_up"].shape[0]
    assert lg == LANES
    halo = 16
    assert taps // 2 <= halo and tr % halo == 0
    guh, gul = _split(p["g_up"])
    per = tr // halo
    nb = n // halo
    row_w = pl.BlockSpec((tr, w), lambda i: (i, 0))
    col = lambda c: pl.BlockSpec((tr, cw), lambda i: (i, c))
    prev = lambda c: pl.BlockSpec((halo, cw), lambda i: (jnp.maximum(i * per - 1, 0), c))
    nxt = lambda c: pl.BlockSpec((halo, cw), lambda i: (jnp.minimum((i + 1) * per, nb - 1), c))
    vec = lambda width: pl.BlockSpec((1, width), lambda i: (0, 0))
    return pl.pallas_call(
        functools.partial(_even_merge_kernel, dims=dims, taps=taps, halo=halo),
        grid=(n // tr,),
        in_specs=[pl.BlockSpec((tr, w), lambda i: (dims.seq_index(i), 0)), row_w, row_w,
                  pl.BlockSpec((tr, LANES), lambda i: (i, 2)),
                  col(0), col(1), prev(0), prev(1), nxt(0), nxt(1),
                  pl.BlockSpec((lg, w), lambda i: (0, 0)), pl.BlockSpec((lg, w), lambda i: (0, 0)),
                  pl.BlockSpec((taps, cw), lambda i: (0, 0)), vec(cw), vec(cw), vec(cw)],
        out_specs=[pl.BlockSpec((tr, w), lambda i: (i, 0)), pl.BlockSpec((tr, cw), lambda i: (i, 0))],
        out_shape=[jax.ShapeDtypeStruct((n, w), BF16), jax.ShapeDtypeStruct((n, cw), BF16)],
        scratch_shapes=[pltpu.VMEM((tr + 2 * halo, cw), F32),
                        pltpu.VMEM((SUBLANES - 1, tr + 2 * halo - SUBLANES, cw), F32)],
        compiler_params=_params(("parallel",)),
        name="even_merge",
    )(yn, bonus0, bonus1, z_lora, z_cv, z_cv, z_cv, z_cv, z_cv, z_cv, guh, gul,
      p["conv_w"], p["conv_b"].reshape(1, cw), p["ln_g"].reshape(1, cw), p["ln_b"].reshape(1, cw))


def even_mixer(dims, x, g_norm, mod, p, hd, n_out_rows):
    b, s, c, n = dims.b, dims.s, dims.c, dims.n
    w = p["w0"].shape[-1]
    cw = p["conv_w"].shape[1]
    lw_w, la_w, lg = p["w_up"].shape[-2], p["a_up"].shape[-2], p["g_up"].shape[0]
    w_in = p["w_in"]
    o = 3 * w
    lora_cols = w_in[:, o:o + 2 * lw_w + 2 * la_w + lg]
    lora_pad = jnp.pad(lora_cols, ((0, 0), (0, -lora_cols.shape[1] % (4 * LANES))))
    rwkv_in = o + 2 * lw_w + 2 * la_w + lg
    z_rkv = nm_matmul(dims, x, g_norm, mod["sh1"], mod["sc1"], w_in[:, :o].astype(BF16), n)
    z_lora = nm_matmul(dims, x, g_norm, mod["sh1"], mod["sc1"], lora_pad.astype(BF16), n)
    z_cv = nm_matmul(dims, x, g_norm, mod["sh1"], mod["sc1"], w_in[:, rwkv_in:].astype(BF16), n)

    heads = w // hd
    t_all = c + s

    def to_scan(a, t_len):
        return a.reshape(b, t_len, heads, hd).transpose(1, 3, 0, 2).reshape(t_len, hd, b * heads)

    def head_tab(vec):
        return jnp.tile(vec.reshape(heads, hd).T[:, None, :], (1, b, 1)).reshape(hd, b * heads)

    rkv_ctx = [to_scan(z_rkv[dims.n_lat:, q * w:(q + 1) * w], c) for q in range(3)]
    rkv_lat = [to_scan(z_rkv[:dims.n_lat, q * w:(q + 1) * w], s) for q in range(3)]
    per_dir = [rwkv_prep(dims, z_rkv, z_lora, p, d_idx, hd) for d_idx in range(2)]
    ys = []
    for d_idx in range(2):
        mu = p["mu"][d_idx]
        tabs = [head_tab(mu[q * w:(q + 1) * w]) for q in range(3)] + [head_tab(p["k_k"]), head_tab(p["k_a"])]
        ys.append(rwkv_scan(rkv_ctx, rkv_lat, to_scan(per_dir[d_idx][0], t_all),
                            to_scan(per_dir[d_idx][1], t_all), tabs, d_idx == 1))

    yn = rwkv_group_norm(ys[0], ys[1], head_tab(p["gn_g"])[None], head_tab(p["gn_b"])[None])
    yn = yn.reshape(t_all, hd, b, heads).transpose(2, 0, 3, 1).reshape(b * t_all, w)
    a_out, b_out = even_merge(dims, yn, per_dir[0][2], per_dir[1][2], z_lora, z_cv, p)
    w_out = p["w_out"].astype(BF16)
    return matmul_residual(dims, a_out, b_out, w_out[:w], w_out[w:], x, mod["g1"], n_out_rows)


def _qkv_prep_kernel(zq_ref, zk_ref, zv_ref, cos_ref, sin_ref, gq_ref, gk_ref, qm_ref,
                     q_ref, k_ref, v_ref, *, hd):
    qm = qm_ref[...]
    cos = cos_ref[...]
    sin = sin_ref[...]
    lane = lax.broadcasted_iota(jnp.int32, (1, LANES), 1)
    low = (lane % hd) < hd // 2

    def norm_rope(z, gain, scale):
        ms = _seg_sum(z * z, qm) * (1.0 / hd)
        y = z * lax.rsqrt(ms + RMS_EPS) * gain
        outs = []
        for c in range(y.shape[1] // LANES):
            yc = y[:, c * LANES:(c + 1) * LANES]
            partner = jnp.where(low, pltpu.roll(yc, LANES - hd // 2, axis=1), pltpu.roll(yc, hd // 2, axis=1))
            outs.append((yc * cos + partner * sin) * scale)
        return jnp.concatenate(outs, axis=1) if len(outs) > 1 else outs[0]

    q_ref[...] = norm_rope(zq_ref[...], gq_ref[...], hd ** -0.5 * math.log2(math.e)).T.astype(BF16)
    k_ref[...] = norm_rope(zk_ref[...], gk_ref[...], 1.0).astype(BF16)
    v_ref[...] = zv_ref[...].astype(BF16)


def qkv_prep(dims, z_main, z_kv, qn_g, kn_g, aw, kw, hd):
    tr, n, s = dims.tr, dims.n, dims.s
    half = hd // 2
    pairs = half // 2
    t = jnp.arange(s)
    inv = ROPE_THETA ** (-jnp.arange(pairs, dtype=F32) / pairs)
    ang = jnp.concatenate([(t // GRID_W).astype(F32)[:, None] * inv, (t % GRID_W).astype(F32)[:, None] * inv], -1)
    cos, sin = jnp.cos(ang), jnp.sin(ang)
    reps = LANES // hd
    cos_t = jnp.tile(jnp.concatenate([cos, cos], -1), (1, reps))
    sin_t = jnp.tile(jnp.concatenate([-sin, sin], -1), (1, reps))
    cos_t = jnp.concatenate([cos_t, jnp.ones((tr, LANES), F32)], axis=0)
    sin_t = jnp.concatenate([sin_t, jnp.zeros((tr, LANES), F32)], axis=0)
    per_seq = s // tr
    n_lat_tiles = dims.n_lat // tr
    tab = pl.BlockSpec((tr, LANES), lambda i: (jnp.where(i < n_lat_tiles, i % per_seq, per_seq), 0))
    assert kw % LANES == 0 and aw % kw == 0
    return pl.pallas_call(
        functools.partial(_qkv_prep_kernel, hd=hd),
        grid=(n // tr,),
        in_specs=[pl.BlockSpec((tr, aw), lambda i: (i, 0)),
                  pl.BlockSpec((tr, kw), lambda i: (i, 0)),
                  pl.BlockSpec((tr, kw), lambda i: (i, 1)),
                  tab, tab,
                  pl.BlockSpec((1, aw), lambda i: (0, 0)), pl.BlockSpec((1, kw), lambda i: (0, 0)),
                  pl.BlockSpec((LANES, LANES), lambda i: (0, 0))],
        out_specs=[pl.BlockSpec((aw, tr), lambda i: (0, i)), pl.BlockSpec((tr, kw), lambda i: (i, 0)),
                   pl.BlockSpec((tr, kw), lambda i: (i, 0))],
        out_shape=[jax.ShapeDtypeStruct((aw, n), BF16), jax.ShapeDtypeStruct((n, kw), BF16),
                   jax.ShapeDtypeStruct((n, kw), BF16)],
        compiler_params=_params(("parallel",)),
        name="qkv_prep",
    )(z_main, z_kv, z_kv, cos_t, sin_t, jnp.tile(qn_g, aw // hd).reshape(1, aw),
      jnp.tile(kn_g, kw // hd).reshape(1, kw), _block_ones(hd))


def _flash_kernel(*refs, kvh, grp, hd, tq, tk, n_lat_chunks):
    if n_lat_chunks:
        qt_ref, kc_ref, vtc_ref, kl_ref, vtl_ref, o_ref, qz_scr, m_scr, l_scr, acc_scr = refs
    else:
        qt_ref, kc_ref, vtc_ref, o_ref, qz_scr, m_scr, l_scr, acc_scr = refs
    qz_scr[...] = jnp.zeros_like(qz_scr)
    for g in range(kvh):
        for u in range(grp):
            h = g * grp + u
            qz_scr[g, g * hd:(g + 1) * hd, u * tq:(u + 1) * tq] = qt_ref[h * hd:(h + 1) * hd, :]
    m_scr[...] = jnp.full_like(m_scr, -jnp.inf)
    l_scr[...] = jnp.zeros_like(l_scr)
    acc_scr[...] = jnp.zeros_like(acc_scr)

    def chunk(k, vt_of):
        sts = [_dot(k, qz_scr[g]) for g in range(kvh)]
        pts = []
        for g in range(kvh):
            m_prev = m_scr[g]
            m_new = jnp.maximum(m_prev, jnp.max(sts[g], axis=0, keepdims=True))
            alpha = jnp.exp2(m_prev - m_new)
            pt = jnp.exp2(sts[g] - m_new)
            l_scr[g] = alpha * l_scr[g] + jnp.sum(pt, axis=0, keepdims=True)
            acc_scr[g] = alpha * acc_scr[g]
            m_scr[g] = m_new
            pts.append(pt.astype(BF16))
        for g in range(kvh):
            acc_scr[g] = acc_scr[g] + _dot(vt_of(g), pts[g])

    chunk(kc_ref[...], lambda g: vtc_ref[0, g * hd:(g + 1) * hd, :])
    if n_lat_chunks:
        def body(j, carry):
            off = pl.multiple_of(j * tk, tk)
            chunk(kl_ref[pl.ds(off, tk), :], lambda g: vtl_ref[j, g * hd:(g + 1) * hd, :])
            return carry

        lax.fori_loop(0, n_lat_chunks, body, 0)

    for g in range(kvh):
        out = acc_scr[g] * (1.0 / l_scr[g])
        for u in range(grp):
            h = g * grp + u
            o_ref[h * hd:(h + 1) * hd, :] = out[:, u * tq:(u + 1) * tq].astype(BF16)


def flash_attention(dims, qt, k, vt_ctx, vt_lat, hd, q_tile0, n_q_tiles, n_lat_chunks):
    aw, kw = qt.shape[0], k.shape[1]
    tq = tk = dims.tr
    kvh = kw // hd
    grp = aw // kw
    ctx0 = dims.n_lat // dims.c
    in_specs = [pl.BlockSpec((aw, tq), lambda bi, qi: (0, q_tile0 + bi * n_q_tiles + qi)),
                pl.BlockSpec((dims.c, kw), lambda bi, qi: (ctx0 + bi, 0)),
                pl.BlockSpec((1, kw, dims.c), lambda bi, qi: (bi, 0, 0))]
    args = [qt, k, vt_ctx]
    if n_lat_chunks:
        in_specs += [pl.BlockSpec((dims.s, kw), lambda bi, qi: (bi, 0)),
                     pl.BlockSpec((n_lat_chunks, kw, tk), lambda bi, qi: (bi, 0, 0))]
        args += [k, vt_lat]
    return pl.pallas_call(
        functools.partial(_flash_kernel, kvh=kvh, grp=grp, hd=hd, tq=tq, tk=tk, n_lat_chunks=n_lat_chunks),
        grid=(dims.b, n_q_tiles),
        in_specs=in_specs,
        out_specs=pl.BlockSpec((aw, tq), lambda bi, qi: (0, bi * n_q_tiles + qi)),
        out_shape=jax.ShapeDtypeStruct((aw, dims.b * n_q_tiles * tq), BF16),
        scratch_shapes=[pltpu.VMEM((kvh, kw, grp * tq), BF16), pltpu.VMEM((kvh, 1, grp * tq), F32),
                        pltpu.VMEM((kvh, 1, grp * tq), F32), pltpu.VMEM((kvh, hd, grp * tq), F32)],
        compiler_params=_params(("parallel", "parallel")),
        name="flash_attention",
    )(*args)


def attention(dims, qt, k, v, hd, ctx_out):
    b, s, c, tr = dims.b, dims.s, dims.c, dims.tr
    kw = k.shape[1]
    nl = dims.n_lat
    vt_ctx = v[nl:].reshape(b, c, kw).transpose(0, 2, 1)
    vt_lat = v[:nl].reshape(b * (s // tr), tr, kw).transpose(0, 2, 1)
    att = flash_attention(dims, qt, k, vt_ctx, vt_lat, hd, 0, s // tr, s // tr)
    if not ctx_out:
        return att
    att_c = flash_attention(dims, qt, k, vt_ctx, None, hd, nl // tr, c // tr, 0)
    return jnp.concatenate([att, att_c], axis=1)


def _lru_prep_kernel(x_ref, xh_ref, cw_ref, cb_ref, wrh_ref, wrl_ref, brg_ref, wih_ref, wil_ref, big_ref,
                     lam_ref, a_ref, b_ref, x_scr, *, dims, reverse, taps):
    first, last = _seg_edges(dims, pl.program_id(0))
    tr = dims.tr
    if reverse:
        x_scr[0:tr, :] = x_ref[...]
        x_scr[tr:, :] = jnp.where(last, 0.0, xh_ref[...])
        base = 0
    else:
        x_scr[0:8, :] = jnp.where(first, 0.0, xh_ref[...])
        x_scr[8:, :] = x_ref[...]
        base = 8 - (taps - 1)
    xc = jnp.zeros((tr, x_scr.shape[1]), F32) + cb_ref[...]
    for j in range(taps):
        xc = xc + cw_ref[j:j + 1, :] * x_scr[pl.ds(base + j, tr), :]
    cwd = wrh_ref.shape[1]
    rg, ig = [], []
    for c in range(wrh_ref.shape[0]):
        xcc = xc[:, c * cwd:(c + 1) * cwd]
        rg.append(_dot3(xcc, wrh_ref[c], wrl_ref[c]))
        ig.append(_dot3(xcc, wih_ref[c], wil_ref[c]))
    cat = lambda xs: jnp.concatenate(xs, axis=1) if len(xs) > 1 else xs[0]
    rg = _sigmoid(cat(rg) + brg_ref[...])
    ig = _sigmoid(cat(ig) + big_ref[...])
    nl = -lam_ref[...]
    softplus = jnp.maximum(nl, 0.0) + jnp.log1p(jnp.exp(-jnp.abs(nl)))
    log_a = -LRU_C * rg * softplus
    a = jnp.exp(log_a)
    a_ref[...] = a
    b_ref[...] = jnp.sqrt(1.0 - a * a) * (ig * xc)


def _block_diag(wb, cwd):
    nb, bw, _ = wb.shape
    per = cwd // bw
    wb = wb.reshape(nb // per, per, bw, bw)
    eye = jnp.eye(per, dtype=wb.dtype)
    return jnp.einsum("cpij,pq->cpiqj", wb, eye).reshape(nb // per, cwd, cwd)


def lru_prep(dims, z_main, col, p, d_idx):
    reverse = d_idx == 1
    tr, n = dims.tr, dims.n
    taps, lw = p["conv_w"].shape[-2:]
    cwd = min(MXU_DIM, lw)
    wrh, wrl = _split(_block_diag(p["w_rg"][d_idx], cwd))
    wih, wil = _split(_block_diag(p["w_ig"][d_idx], cwd))
    per = tr // 8
    nb = n // 8
    if reverse:
        halo = pl.BlockSpec((8, lw), lambda i: (jnp.minimum((i + 1) * per, nb - 1), col))
    else:
        halo = pl.BlockSpec((8, lw), lambda i: (jnp.maximum(i * per - 1, 0), col))
    vec = pl.BlockSpec((1, lw), lambda i: (0, 0))
    mat = pl.BlockSpec((lw // cwd, cwd, cwd), lambda i: (0, 0, 0))
    return pl.pallas_call(
        functools.partial(_lru_prep_kernel, dims=dims, reverse=reverse, taps=taps),
        grid=(n // tr,),
        in_specs=[pl.BlockSpec((tr, lw), lambda i: (i, col)), halo,
                  pl.BlockSpec((taps, lw), lambda i: (0, 0)), vec, mat, mat, vec, mat, mat, vec, vec],
        out_specs=[pl.BlockSpec((tr, lw), lambda i: (i, 0))] * 2,
        out_shape=[jax.ShapeDtypeStruct((n, lw), F32)] * 2,
        scratch_shapes=[pltpu.VMEM((tr + 8, lw), F32)],
        compiler_params=_params(("parallel",)),
        name="lru_prep",
    )(z_main, z_main, p["conv_w"][d_idx], p["conv_b"][d_idx].reshape(1, lw), wrh, wrl,
      p["b_rg"][d_idx].reshape(1, lw), wih, wil, p["b_ig"][d_idx].reshape(1, lw),
      p["lam"][d_idx].reshape(1, lw))


def _lru_scan_kernel(a_ref, b_ref, h_ref, h_scr, *, tr, reverse):
    @pl.when(pl.program_id(1) == 0)
    def _():
        h_scr[...] = jnp.zeros_like(h_scr)

    n_groups = tr // SUBLANES
    rid = lax.broadcasted_iota(jnp.int32, (SUBLANES, 1), 0)

    def group(gi, h):
        g = n_groups - 1 - gi if reverse else gi
        r0 = pl.multiple_of(g * SUBLANES, SUBLANES)
        a = a_ref[pl.ds(r0, SUBLANES), :]
        b = b_ref[pl.ds(r0, SUBLANES), :]
        for sh in (1, 2, 4):
            if reverse:
                has = rid < SUBLANES - sh
                a_sh = jnp.where(has, pltpu.roll(a, SUBLANES - sh, axis=0), 1.0)
                b_sh = jnp.where(has, pltpu.roll(b, SUBLANES - sh, axis=0), 0.0)
            else:
                has = rid >= sh
                a_sh = jnp.where(has, pltpu.roll(a, sh, axis=0), 1.0)
                b_sh = jnp.where(has, pltpu.roll(b, sh, axis=0), 0.0)
            b = a * b_sh + b
            a = a * a_sh
        hs = a * h + b
        h_ref[pl.ds(r0, SUBLANES), :] = hs
        return hs[0:1] if reverse else hs[SUBLANES - 1:]

    h_scr[...] = lax.fori_loop(0, n_groups, group, h_scr[...])


def lru_scan(dims, a, b, reverse):
    tr, s, bsz = dims.tr, dims.s, dims.b
    lw = a.shape[-1]
    lat_tiles = s // tr
    ctx0 = dims.n_lat // tr

    def tmap(bi, j):
        lat = lat_tiles - j if reverse else j - 1
        return (jnp.where(j == 0, ctx0 + bi, bi * lat_tiles + lat), 0)

    spec = pl.BlockSpec((tr, lw), tmap)
    return pl.pallas_call(
        functools.partial(_lru_scan_kernel, tr=tr, reverse=reverse),
        grid=(bsz, lat_tiles + 1),
        in_specs=[spec, spec],
        out_specs=spec,
        out_shape=jax.ShapeDtypeStruct(a.shape, F32),
        scratch_shapes=[pltpu.VMEM((1, lw), F32)],
        compiler_params=_params(("arbitrary", "arbitrary")),
        name="lru_scan",
    )(a, b)


def _lru_gate_kernel(h0_ref, h1_ref, g_ref, o_ref):
    g = g_ref[...]
    gelu = 0.5 * g * (1.0 + jnp.tanh(math.sqrt(2.0 / math.pi) * (g + 0.044715 * (g * g * g))))
    o_ref[...] = ((h0_ref[...] + h1_ref[...]) * gelu).astype(BF16)


def lru_gate(dims, h0, h1, z_main, col):
    tr, n = dims.tr, dims.n
    lw = h0.shape[-1]
    row = pl.BlockSpec((tr, lw), lambda i: (i, 0))
    return pl.pallas_call(
        _lru_gate_kernel,
        grid=(n // tr,),
        in_specs=[row, row, pl.BlockSpec((tr, lw), lambda i: (i, col))],
        out_specs=row,
        out_shape=jax.ShapeDtypeStruct((n, lw), BF16),
        compiler_params=_params(("parallel",)),
        name="lru_gate",
    )(h0, h1, z_main)


def odd_mixer(dims, x, g_norm, mod, p, hd, ctx_out, n_out_rows):
    n = dims.n
    lw = p["lam"].shape[-1]
    w_in = p["w_in"]
    aw = p["w_out"].shape[0] - lw
    kw = (w_in.shape[1] - aw - 2 * lw) // 2
    assert aw == lw
    main_cols = jnp.concatenate([w_in[:, :aw], w_in[:, aw + 2 * kw:]], axis=1)
    z_main = nm_matmul(dims, x, g_norm, mod["sh1"], mod["sc1"], main_cols.astype(BF16), n)
    z_kv = nm_matmul(dims, x, g_norm, mod["sh1"], mod["sc1"], w_in[:, aw:aw + 2 * kw].astype(BF16), n)
    q, k, v = qkv_prep(dims, z_main, z_kv, p["qn_g"], p["kn_g"], aw, kw, hd)
    att = attention(dims, q, k, v, hd, ctx_out)
    ab = [lru_prep(dims, z_main, 1, p, d_idx) for d_idx in range(2)]
    hs = [lru_scan(dims, ab[d_idx][0], ab[d_idx][1], d_idx == 1) for d_idx in range(2)]
    hg = lru_gate(dims, hs[0], hs[1], z_main, 2)
    w_out = p["w_out"].astype(BF16)
    return matmul_residual(dims, att, hg, w_out[:aw], w_out[aw:], x, mod["g1"], n_out_rows, a1_feature_major=True)


def _moe_kernel(be_ref, na_ref, x_ref, wg_ref, wu_ref, wd_ref, gt_ref, o_ref):
    i = pl.program_id(0)

    @pl.when(i < na_ref[0])
    def _():
        x = x_ref[...]
        hg = _dot(x, wg_ref[0])
        hu = _dot(x, wu_ref[0])
        h = (hg * _sigmoid(hg) * hu).astype(BF16)
        o_ref[...] = (_dot(h, wd_ref[0]) * gt_ref[...]).astype(o_ref.dtype)

    @pl.when(i >= na_ref[0])
    def _():
        o_ref[...] = jnp.zeros_like(o_ref)


def moe_experts(xs, gate_rows, blk_expert, n_active, w_gate, w_up, w_down):
    rows, d = xs.shape
    n_blk = rows // MOE_ROWS
    de = w_gate.shape[-1]
    grid_spec = pltpu.PrefetchScalarGridSpec(
        num_scalar_prefetch=2,
        grid=(n_blk,),
        in_specs=[pl.BlockSpec((MOE_ROWS, d), lambda i, be, na: (i, 0)),
                  pl.BlockSpec((1, d, de), lambda i, be, na: (be[i], 0, 0)),
                  pl.BlockSpec((1, d, de), lambda i, be, na: (be[i], 0, 0)),
                  pl.BlockSpec((1, de, d), lambda i, be, na: (be[i], 0, 0)),
                  pl.BlockSpec((MOE_ROWS, 1), lambda i, be, na: (i, 0))],
        out_specs=pl.BlockSpec((MOE_ROWS, d), lambda i, be, na: (i, 0)),
    )
    return pl.pallas_call(
        _moe_kernel,
        grid_spec=grid_spec,
        out_shape=jax.ShapeDtypeStruct((rows, d), BF16),
        compiler_params=_params(("arbitrary",)),
        name="moe_experts",
    )(blk_expert, n_active, xs, w_gate, w_up, w_down, gate_rows)


def hierarchical_moe(dims, x, g_norm, mod, p, n_rows):
    n_grp = p["w_grp"].shape[1]
    epg = p["w_router"].shape[-1]
    n_exp = n_grp * epg
    d = dims.d
    w_rt = jnp.concatenate([p["w_grp"], p["w_router"].transpose(1, 0, 2).reshape(d, n_exp)], axis=1)
    b_rt = jnp.concatenate([p["b_grp"], p["b_router"].reshape(n_exp)])
    padc = -w_rt.shape[1] % LANES
    f, rt = nm_router(dims, x, g_norm, mod["sh2"], mod["sc2"], jnp.pad(w_rt, ((0, 0), (0, padc))),
                      jnp.pad(b_rt, (0, padc)), n_rows, n_grp, epg)

    gates = rt[:, :TOP_K]
    flat_e = rt[:, TOP_K:2 * TOP_K].astype(jnp.int32).reshape(-1)
    n_assign = n_rows * TOP_K
    i32 = jnp.int32
    order = jnp.argsort(flat_e).astype(i32)
    inv = jnp.argsort(order).astype(i32)
    e_sorted = flat_e[order]
    counts = jnp.sum(flat_e[:, None] == jnp.arange(n_exp, dtype=i32)[None, :], axis=0, dtype=i32)
    padded = (counts + MOE_ROWS - 1) // MOE_ROWS * MOE_ROWS
    seg_start = jnp.cumsum(counts) - counts
    pad_end = jnp.cumsum(padded)
    pad_start = pad_end - padded
    slot = pad_start[e_sorted] + jnp.arange(n_assign, dtype=i32) - seg_start[e_sorted]
    n_blk = -(-n_assign // MOE_ROWS) + n_exp
    blk_expert = jnp.minimum(jnp.sum(jnp.arange(n_blk, dtype=i32)[:, None] >= (pad_end // MOE_ROWS)[None, :],
                                     axis=1, dtype=i32), n_exp - 1)
    n_active = (pad_end[-1] // MOE_ROWS).astype(i32).reshape(1)
    rank = (jnp.arange(n_blk, dtype=i32) * MOE_ROWS - pad_start[blk_expert])[:, None] \
        + jnp.arange(MOE_ROWS, dtype=i32)[None, :]
    valid = (rank < counts[blk_expert][:, None]).reshape(-1)
    pos = jnp.clip(seg_start[blk_expert][:, None] + rank, 0, n_assign - 1).reshape(-1)
    src = jnp.where(valid, order[pos] // TOP_K, 0)
    gate_rows = jnp.where(valid, gates.reshape(-1)[order][pos], 0.0)
    slot_of = slot[inv].reshape(n_rows, TOP_K)

    xs = jnp.take(f, src, axis=0, mode="clip")
    out = moe_experts(xs, gate_rows[:, None], blk_expert, n_active, p["w_gate"].astype(BF16),
                      p["w_up"].astype(BF16), p["w_down"].astype(BF16))
    y0 = jnp.take(out, slot_of[:, 0], axis=0, mode="clip")
    y1 = jnp.take(out, slot_of[:, 1], axis=0, mode="clip")
    return combine_residual(dims, x, y0, y1, mod["g2"], n_rows)


def kernel(x, c, ctx, c_ctx, norm_mix_g, norm_ffn_g, ada_w, ada_b, even_w_in, rwkv_mu, rwkv_w0, rwkv_w_up, rwkv_a0, rwkv_a_up, rwkv_k_k, rwkv_k_a, rwkv_r_k, rwkv_g_up, rwkv_gn_g, rwkv_gn_b, conv_w, conv_b, conv_ln_g, conv_ln_b, even_w_out, odd_w_in, att_qn_g, att_kn_g, lru_conv_w, lru_conv_b, lru_w_rg, lru_b_rg, lru_w_ig, lru_b_ig, lru_lambda, odd_w_out, moe_w_grp, moe_b_grp, moe_w_router, moe_b_router, moe_w_gate, moe_w_up, moe_w_down, final_g):
    bsz, s, d = x.shape
    n_ctx = ctx.shape[1]
    depth = norm_mix_g.shape[0]
    hd = att_qn_g.shape[-1]
    dims = _Dims(bsz, s, n_ctx, d)
    xa = jnp.concatenate([x.reshape(bsz * s, d), ctx.reshape(bsz * n_ctx, d)], axis=0)

    cond = jnp.concatenate([c, c_ctx[None, :]], axis=0)
    cond = cond * _sigmoid(cond)
    rows = -(bsz + 1) % 8
    cond = jnp.pad(cond, ((0, rows), (0, 0)))

    for layer in range(depth):
        ctx_out = layer < depth - 1
        n_rows = dims.n if ctx_out else dims.n_lat
        j = layer // 2
        mod_all = ada_matmul(cond, ada_w[layer], ada_b[layer])[:bsz + 1]
        names = ("sh1", "sc1", "g1", "sh2", "sc2", "g2")
        mod = {nm: mod_all[:, q * d:(q + 1) * d].reshape(bsz + 1, 1, d) for q, nm in enumerate(names)}
        if layer % 2 == 0:
            p = dict(w_in=even_w_in[j], mu=rwkv_mu[j], w0=rwkv_w0[j], w_up=rwkv_w_up[j], a0=rwkv_a0[j],
                     a_up=rwkv_a_up[j], k_k=rwkv_k_k[j], k_a=rwkv_k_a[j], r_k=rwkv_r_k[j].reshape(-1),
                     g_up=rwkv_g_up[j], gn_g=rwkv_gn_g[j], gn_b=rwkv_gn_b[j], conv_w=conv_w[j],
                     conv_b=conv_b[j], ln_g=conv_ln_g[j], ln_b=conv_ln_b[j], w_out=even_w_out[j])
            x_new = even_mixer(dims, xa, norm_mix_g[layer], mod, p, hd, n_rows)
        else:
            p = dict(w_in=odd_w_in[j], qn_g=att_qn_g[j], kn_g=att_kn_g[j], conv_w=lru_conv_w[j],
                     conv_b=lru_conv_b[j], w_rg=lru_w_rg[j], b_rg=lru_b_rg[j].reshape(2, -1),
                     w_ig=lru_w_ig[j], b_ig=lru_b_ig[j].reshape(2, -1), lam=lru_lambda[j], w_out=odd_w_out[j])
            x_new = odd_mixer(dims, xa, norm_mix_g[layer], mod, p, hd, ctx_out, n_rows)
        pm = dict(w_grp=moe_w_grp[layer], b_grp=moe_b_grp[layer], w_router=moe_w_router[layer],
                  b_router=moe_b_router[layer], w_gate=moe_w_gate[layer], w_up=moe_w_up[layer],
                  w_down=moe_w_down[layer])
        xa = hierarchical_moe(dims, x_new, norm_ffn_g[layer], mod, pm, n_rows)
    return final_norm(dims, xa, final_g, dims.n_lat).reshape(bsz, s, d)
```
